```python
import math
import jax
import jax.numpy as jnp
from jax import lax
import numpy as np

D_MODEL = 2048
BATCH = 1
SEQ = 8192
DEPTH = 4

CTX_LEN = 256
GRID_W = 64
HEAD_DIM = 128
N_BRANCH = 4
BRANCH_WIDTH = D_MODEL // 2

A_HEADS = BRANCH_WIDTH // HEAD_DIM
A_KV_HEADS = A_HEADS // 4
A_WINDOW = 128
A_BLOCK = 128

B_HEADS = BRANCH_WIDTH // HEAD_DIM
B_CHUNK = 64
SHORT_CONV = 3

C_WIDTH = BRANCH_WIDTH
HYENA_EMB = 33
HYENA_FFN = 64
HYENA_FAST_DECAY = 0.3
HYENA_SLOW_DECAY = 1.5
HYENA_TARGET = 1e-2

D_HEADS = BRANCH_WIDTH // HEAD_DIM
NA_KH_MAX = 8
NA_KW = 16

N_EXPERTS = 16
EXPERT_FF = D_MODEL // 2
EC_CAPACITY = 2

ROPE_BASE = 10000.0
LN_EPS = 1e-6
NORM_EPS = 1e-6
NEG_INF = -1e30

A_Q_W = A_HEADS * HEAD_DIM
A_KV_W = A_KV_HEADS * HEAD_DIM
B_W = B_HEADS * HEAD_DIM
D_W = D_HEADS * HEAD_DIM
GATE_W = N_BRANCH * D_MODEL
IN_SIZES = (A_Q_W, A_KV_W, A_KV_W, B_W, B_W, B_W, B_W, 2 * B_HEADS, 2 * B_HEADS,
            C_WIDTH, C_WIDTH, C_WIDTH, D_W, D_W, D_W, GATE_W)
IN_WIDTH = A_Q_W + 2 * A_KV_W + 4 * B_W + 4 * B_HEADS + 3 * C_WIDTH + 3 * D_W + GATE_W

kernel_name = 'hybrid_diffusion_backbone'


def layer_norm(x, g, b):
    xf = x.astype(jnp.float32)
    mu = jnp.mean(xf, axis=-1, keepdims=True)
    var = jnp.mean(jnp.square(xf - mu), axis=-1, keepdims=True)
    return ((xf - mu) * lax.rsqrt(var + LN_EPS) * g + b).astype(x.dtype)


def l2norm(x):
    return x * lax.rsqrt(jnp.sum(jnp.square(x), axis=-1, keepdims=True) + NORM_EPS)


def split_cols(p):
    return jnp.split(p, np.cumsum(IN_SIZES)[:-1].tolist(), axis=-1)


def to_heads(t, n_heads):
    return t.reshape(t.shape[0], t.shape[1], n_heads, HEAD_DIM)


def depthwise_conv(x, w):
    k, ch = w.shape
    return lax.conv_general_dilated(x, w.astype(x.dtype)[:, None, :], window_strides=(1,),
                                    padding=[(k // 2, k - 1 - k // 2)],
                                    dimension_numbers=('NWC', 'WIO', 'NWC'),
                                    feature_group_count=ch)


def axial_rope(x, pos_row, pos_col):
    half = HEAD_DIM // 2
    nf = half // 2
    inv_freq = ROPE_BASE ** (-jnp.arange(nf, dtype=jnp.float32) / nf)

    def rot(xp, pos):
        ang = pos.astype(jnp.float32)[:, None] * inv_freq[None, :]
        cos = jnp.cos(ang)[None, :, None, :]
        sin = jnp.sin(ang)[None, :, None, :]
        x1 = xp[..., :nf].astype(jnp.float32)
        x2 = xp[..., nf:].astype(jnp.float32)
        return jnp.concatenate([x1 * cos - x2 * sin, x2 * cos + x1 * sin], axis=-1)

    return jnp.concatenate([rot(x[..., :half], pos_row), rot(x[..., half:], pos_col)],
                           axis=-1).astype(x.dtype)


def sink_softmax(s, sink):
    m = jnp.maximum(jnp.max(s, axis=-1, keepdims=True), sink)
    p = jnp.exp(s - m)
    return p / (jnp.sum(p, axis=-1, keepdims=True) + jnp.exp(sink - m))


def windowed_gqa(q_l, k_l, v_l, q_c, k_c, v_c, sink):
    bn, n = q_l.shape[:2]
    lc = k_c.shape[1]
    grp = A_HEADS // A_KV_HEADS
    scale = HEAD_DIM ** -0.5
    nb = n // A_BLOCK
    sink_g = sink.astype(jnp.float32).reshape(A_KV_HEADS, grp)[:, :, None, None]
    qb = q_l.reshape(bn, nb, A_BLOCK, A_KV_HEADS, grp, HEAD_DIM)

    def band(t):
        tb = t.reshape(bn, nb, A_BLOCK, A_KV_HEADS, HEAD_DIM)
        z = jnp.zeros_like(tb[:, :1])
        return jnp.concatenate([jnp.concatenate([z, tb[:, :-1]], axis=1), tb,
                                jnp.concatenate([tb[:, 1:], z], axis=1)], axis=2)

    k_win, v_win = band(k_l), band(v_l)
    qpos = jnp.arange(A_BLOCK)
    kpos = jnp.arange(3 * A_BLOCK) - A_BLOCK
    kabs = jnp.arange(nb)[:, None] * A_BLOCK + kpos[None, :]
    mask = ((jnp.abs(kpos[None, :] - qpos[:, None]) <= A_WINDOW)[None]
            & ((kabs >= 0) & (kabs < n))[:, None, :])
    s_loc = jnp.einsum('bnqhgd,bnkhd->bnhgqk', qb, k_win).astype(jnp.float32) * scale
    s_loc = jnp.where(mask[None, :, None, None], s_loc, NEG_INF)
    s_ctx = jnp.einsum('bnqhgd,bkhd->bnhgqk', qb, k_c).astype(jnp.float32) * scale
    p = sink_softmax(jnp.concatenate([s_ctx, s_loc], axis=-1), sink_g).astype(v_l.dtype)
    y_l = (jnp.einsum('bnhgqk,bkhd->bnqhgd', p[..., :lc], v_c)
           + jnp.einsum('bnhgqk,bnkhd->bnqhgd', p[..., lc:], v_win)).reshape(bn, n, A_Q_W)
    if q_c is None:
        return y_l, None
    qc = q_c.reshape(bn, lc, A_KV_HEADS, grp, HEAD_DIM)
    s_cc = jnp.einsum('bqhgd,bkhd->bhgqk', qc, k_c).astype(jnp.float32) * scale
    p_c = sink_softmax(s_cc, sink_g).astype(v_c.dtype)
    y_c = jnp.einsum('bhgqk,bkhd->bqhgd', p_c, v_c).reshape(bn, lc, A_Q_W)
    return y_l, y_c


def delta_chunked(q, k, v, g, beta, state0):
    bn, nh, n, dk = q.shape
    dv = v.shape[-1]
    nc = n // B_CHUNK
    q = q * dk ** -0.5
    kb = k * beta[..., None]
    vb = v * beta[..., None]
    ch = lambda t: t.reshape(bn, nh, nc, B_CHUNK, t.shape[-1])
    q, k, kb, vb = ch(q), ch(k), ch(kb), ch(vb)
    g = jnp.cumsum(g.reshape(bn, nh, nc, B_CHUNK), axis=-1)
    causal = jnp.tril(jnp.ones((B_CHUNK, B_CHUNK), dtype=bool))
    strict = jnp.tril(jnp.ones((B_CHUNK, B_CHUNK), dtype=bool), -1)
    decay = jnp.exp(jnp.where(causal, g[..., :, None] - g[..., None, :], NEG_INF))
    a_mat = jnp.where(strict, jnp.einsum('bhnid,bhnjd->bhnij', kb, k) * decay, 0.0)
    rhs = jnp.concatenate([vb, kb * jnp.exp(g)[..., None]], axis=-1)
    sol = lax.linalg.triangular_solve(jnp.eye(B_CHUNK, dtype=jnp.float32) + a_mat, rhs,
                                      left_side=True, lower=True, unit_diagonal=True)
    u, w = sol[..., :dv], sol[..., dv:]
    qk = jnp.where(causal, jnp.einsum('bhnid,bhnjd->bhnij', q, k) * decay, 0.0)
    qg = q * jnp.exp(g)[..., None]
    kd = k * jnp.exp(g[..., -1:] - g)[..., None]
    glast = jnp.exp(g[..., -1])
    xs = tuple(jnp.moveaxis(t, 2, 0) for t in (qg, kd, u, w, qk, glast))

    def step(s, inp):
        qg_i, kd_i, u_i, w_i, qk_i, gl_i = inp
        v_new = u_i - jnp.einsum('bhck,bhkv->bhcv', w_i, s)
        o = jnp.einsum('bhck,bhkv->bhcv', qg_i, s) + jnp.einsum('bhcj,bhjv->bhcv', qk_i, v_new)
        s = s * gl_i[..., None, None] + jnp.einsum('bhck,bhcv->bhkv', kd_i, v_new)
        return s, o

    s_fin, o = lax.scan(step, state0, xs)
    return jnp.moveaxis(o, 0, 2).reshape(bn, nh, n, dv), s_fin


def gated_deltanet(parts_l, parts_c, conv_w, a_log, dt_bias, norm_g, ctx_out):
    def prep(parts):
        q, k, v, gate, a, b = parts
        bn, n = q.shape[:2]
        qkv = jax.nn.silu(depthwise_conv(jnp.concatenate([q, k, v], axis=-1), conv_w))
        q, k, v = jnp.split(qkv.astype(jnp.float32), 3, axis=-1)
        heads = lambda t: jnp.transpose(to_heads(t, B_HEADS), (0, 2, 1, 3))
        a = a.astype(jnp.float32).reshape(bn, n, 2, B_HEADS)
        g = -jnp.exp(a_log.astype(jnp.float32)) * jax.nn.softplus(a + dt_bias.astype(jnp.float32))
        beta = jax.nn.sigmoid(b.astype(jnp.float32).reshape(bn, n, 2, B_HEADS))
        return (l2norm(heads(q)), l2norm(heads(k)), heads(v),
                jnp.transpose(g, (2, 0, 3, 1)), jnp.transpose(beta, (2, 0, 3, 1)), gate)

    q_l, k_l, v_l, g_l, be_l, gate_l = prep(parts_l)
    q_c, k_c, v_c, g_c, be_c, gate_c = prep(parts_c)
    flip = lambda t: jnp.flip(t, axis=2)
    s0 = jnp.zeros((q_l.shape[0], B_HEADS, HEAD_DIM, HEAD_DIM), jnp.float32)
    oc_f, st_f = delta_chunked(q_c, k_c, v_c, g_c[0], be_c[0], s0)
    ol_f, _ = delta_chunked(q_l, k_l, v_l, g_l[0], be_l[0], st_f)
    oc_b, st_b = delta_chunked(flip(q_c), flip(k_c), flip(v_c), flip(g_c[1]), flip(be_c[1]), s0)
    ol_b, _ = delta_chunked(flip(q_l), flip(k_l), flip(v_l), flip(g_l[1]), flip(be_l[1]), st_b)

    def out(o, gate):
        o = jnp.transpose(o, (0, 2, 1, 3))
        o = o * lax.rsqrt(jnp.mean(jnp.square(o), axis=-1, keepdims=True) + NORM_EPS) * norm_g.astype(jnp.float32)
        o = o * jax.nn.silu(to_heads(gate, B_HEADS).astype(jnp.float32))
        return o.reshape(o.shape[0], o.shape[1], B_W).astype(gate.dtype)

    y_l = out(ol_f + flip(ol_b), gate_l)
    y_c = out(oc_f + flip(oc_b), gate_c) if ctx_out else None
    return y_l, y_c


def hyena_filter(n, w1, b1, freq, w2, b2, w3, b3, w4, b4):
    t = jnp.linspace(0.0, 1.0, n, dtype=jnp.float32)[:, None]
    bands = (HYENA_EMB - 1) // 2
    wpos = 2.0 * math.pi * jnp.arange(n, dtype=jnp.float32)[:, None] / n
    f = jnp.linspace(1e-4, bands - 1, bands, dtype=jnp.float32)[None, :]
    z = jnp.concatenate([t, jnp.cos(f * wpos), -jnp.sin(f * wpos)], axis=-1)
    fr = freq.astype(jnp.float32)
    h = jnp.sin(fr * (z @ w1.astype(jnp.float32) + b1.astype(jnp.float32)))
    h = jnp.sin(fr * (h @ w2.astype(jnp.float32) + b2.astype(jnp.float32)))
    h = jnp.sin(fr * (h @ w3.astype(jnp.float32) + b3.astype(jnp.float32)))
    h = h @ w4.astype(jnp.float32) + b4.astype(jnp.float32)
    max_decay = math.log(HYENA_TARGET) / HYENA_FAST_DECAY
    min_decay = math.log(HYENA_TARGET) / HYENA_SLOW_DECAY
    deltas = jnp.linspace(min_decay, max_decay, C_WIDTH, dtype=jnp.float32)
    window = jnp.exp(-t * jnp.abs(deltas)[None, :])
    h = h.reshape(n, 2, C_WIDTH) * window[:, None, :]
    return h[:, 0], h[:, 1]


def bidir_fftconv(z, h_f, h_b, d_bias):
    n = z.shape[1]
    k = jnp.concatenate([h_f, jnp.zeros_like(h_f[:1]), jnp.flip(h_b[1:], axis=0)], axis=0)
    kf = jnp.fft.rfft(k, axis=0)
    zf = jnp.fft.rfft(z, n=2 * n, axis=1)
    y = jnp.fft.irfft(zf * kf[None], n=2 * n, axis=1)[:, :n]
    return y + z * d_bias.astype(jnp.float32)


def hyena(x0, x1, v, conv_w, filt, d_bias):
    n = v.shape[1]
    xc = depthwise_conv(jnp.concatenate([x0, x1, v], axis=-1), conv_w)
    x0, x1, v = jnp.split(xc, 3, axis=-1)
    h_f, h_b = hyena_filter(n, *filt)
    y = bidir_fftconv((v * x1).astype(jnp.float32), h_f, h_b, d_bias)
    return (y * x0.astype(jnp.float32)).astype(x0.dtype)


def neighbourhood_attention(q_l, k_l, v_l, q_c, k_c, v_c, rel_bias):
    bn, n, nh, d = q_l.shape
    lc = k_c.shape[1]
    rows = n // GRID_W
    kh = min(NA_KH_MAX, rows)
    kw = NA_KW
    scale = HEAD_DIM ** -0.5
    qg = q_l.reshape(bn, rows, GRID_W, nh, d)
    kg = k_l.reshape(bn, rows, GRID_W, nh, d)
    vg = v_l.reshape(bn, rows, GRID_W, nh, d)
    r = jnp.arange(rows)
    key_rows = jnp.clip(r - kh // 2, 0, rows - kh)[:, None] + jnp.arange(kh)[None, :]
    k_rows = kg[:, key_rows]
    v_rows = vg[:, key_rows]
    cq = jnp.arange(GRID_W)
    col_start = jnp.clip(cq - kw // 2, 0, GRID_W - kw)
    col_ok = (cq[None, :] >= col_start[:, None]) & (cq[None, :] < col_start[:, None] + kw)
    dr = key_rows - r[:, None] + (NA_KH_MAX - 1)
    dc = jnp.clip(cq[None, :] - cq[:, None], -(kw - 1), kw - 1) + (kw - 1)
    bias = rel_bias.astype(jnp.float32)[:, :, dc][:, dr]
    bias = jnp.transpose(bias, (1, 0, 3, 2, 4))
    s_loc = jnp.einsum('brqhd,brjkhd->brhqjk', qg, k_rows).astype(jnp.float32) * scale + bias[None]
    s_loc = jnp.where(col_ok[:, None, :], s_loc, NEG_INF).reshape(bn, rows, nh, GRID_W, kh * GRID_W)
    s_ctx = jnp.einsum('brqhd,bkhd->brhqk', qg, k_c).astype(jnp.float32) * scale
    p = jax.nn.softmax(jnp.concatenate([s_ctx, s_loc], axis=-1), axis=-1).astype(v_l.dtype)
    p_loc = p[..., lc:].reshape(bn, rows, nh, GRID_W, kh, GRID_W)
    y_l = (jnp.einsum('brhqk,bkhd->brqhd', p[..., :lc], v_c)
           + jnp.einsum('brhqjk,brjkhd->brqhd', p_loc, v_rows)).reshape(bn, n, D_W)
    if q_c is None:
        return y_l, None
    s_cc = jnp.einsum('bqhd,bkhd->bhqk', q_c, k_c).astype(jnp.float32) * scale
    p_c = jax.nn.softmax(s_cc, axis=-1).astype(v_c.dtype)
    y_c = jnp.einsum('bhqk,bkhd->bqhd', p_c, v_c).reshape(bn, lc, D_W)
    return y_l, y_c


def merge_branches(ys, gate_logits, w_branch, w_o):
    ys = jnp.stack(ys, axis=2)
    proj = jnp.einsum('bnif,ifd->bnid', ys, w_branch)
    gates = jax.nn.sigmoid(gate_logits.reshape(gate_logits.shape[0], gate_logits.shape[1], N_BRANCH, D_MODEL))
    return jnp.sum(gates * proj, axis=2) @ w_o


def hybrid_mixer(u_l, u_c, pos_row, pos_col, w_in, sink, d_conv, a_log, dt_bias, d_norm_g,
                 h_conv, h_filt, h_d, rel_bias, w_branch, w_o, ctx_out):
    pl = split_cols(u_l @ w_in)
    pc = split_cols(u_c @ w_in)
    aq_l = axial_rope(to_heads(pl[0], A_HEADS), pos_row, pos_col)
    ak_l = axial_rope(to_heads(pl[1], A_KV_HEADS), pos_row, pos_col)
    ya_l, ya_c = windowed_gqa(aq_l, ak_l, to_heads(pl[2], A_KV_HEADS),
                              to_heads(pc[0], A_HEADS) if ctx_out else None,
                              to_heads(pc[1], A_KV_HEADS), to_heads(pc[2], A_KV_HEADS), sink)
    yb_l, yb_c = gated_deltanet(pl[3:9], pc[3:9], d_conv, a_log, dt_bias, d_norm_g, ctx_out)
    yc_l = hyena(pl[9], pl[10], pl[11], h_conv, h_filt, h_d)
    yd_l, yd_c = neighbourhood_attention(to_heads(pl[12], D_HEADS), to_heads(pl[13], D_HEADS),
                                         to_heads(pl[14], D_HEADS),
                                         to_heads(pc[12], D_HEADS) if ctx_out else None,
                                         to_heads(pc[13], D_HEADS), to_heads(pc[14], D_HEADS), rel_bias)
    out_l = merge_branches((ya_l, yb_l, yc_l, yd_l), pl[15], w_branch, w_o)
    if not ctx_out:
        return out_l, None
    yc_c = hyena(pc[9], pc[10], pc[11], h_conv, h_filt, h_d)
    out_c = merge_branches((ya_c, yb_c, yc_c, yd_c), pc[15], w_branch, w_o)
    return out_l, out_c


def expert_choice_ffn(u, w_router, w1, w3, w2):
    bn, n, dm = u.shape
    cap = EC_CAPACITY * n // N_EXPERTS
    aff = jax.nn.softmax((u @ w_router).astype(jnp.float32), axis=-1)
    top_w, top_idx = lax.top_k(jnp.swapaxes(aff, 1, 2), cap)
    xg = jax.vmap(lambda ub, ib: ub[ib])(u, top_idx)
    hdn = jax.nn.silu(jnp.einsum('becd,edf->becf', xg, w1)) * jnp.einsum('becd,edf->becf', xg, w3)
    y = jnp.einsum('becf,efd->becd', hdn, w2) * top_w[..., None].astype(u.dtype)
    return jax.vmap(lambda yb, ib: jnp.zeros((n, dm), yb.dtype).at[ib.reshape(-1)].add(yb.reshape(-1, dm)))(y, top_idx)


def setup_inputs(seed: int = 0) -> dict:
    key = jax.random.key(seed)
    ks = iter(jax.random.split(key, 48))
    f32 = jnp.float32

    def nrm(shape, scale):
        return jax.random.normal(next(ks), shape, f32) * scale

    beta_dn = (8.0 * DEPTH) ** -0.25
    a_log = jnp.log(jax.random.uniform(next(ks), (DEPTH, 2, B_HEADS), f32, 1.0, 16.0))
    dt = jnp.exp(jax.random.uniform(next(ks), (DEPTH, 2, B_HEADS), f32, math.log(1e-3), math.log(1e-1)))
    dt_bias = dt + jnp.log(-jnp.expm1(-dt))
    return {
        'x': nrm((BATCH, SEQ, D_MODEL), 1.0),
        'c': nrm((BATCH, D_MODEL), 1.0),
        'ctx': nrm((BATCH, CTX_LEN, D_MODEL), 1.0),
        'c_ctx': nrm((D_MODEL,), 1.0),
        'w_mod': nrm((DEPTH, D_MODEL, 6 * D_MODEL), 0.5 * D_MODEL ** -0.5),
        'b_mod': nrm((DEPTH, 6 * D_MODEL), 0.02),
        'w_in': nrm((DEPTH, D_MODEL, IN_WIDTH), D_MODEL ** -0.5),
        'attn_sink': nrm((DEPTH, A_HEADS), 0.5),
        'delta_conv': nrm((DEPTH, SHORT_CONV, 3 * B_W), SHORT_CONV ** -0.5),
        'delta_a_log': a_log,
        'delta_dt_bias': dt_bias,
        'delta_norm_g': 1.0 + nrm((DEPTH, HEAD_DIM), 0.02),
        'hyena_conv': nrm((DEPTH, SHORT_CONV, 3 * C_WIDTH), SHORT_CONV ** -0.5),
        'hyena_w1': nrm((DEPTH, HYENA_EMB, HYENA_FFN), HYENA_EMB ** -0.5),
        'hyena_b1': nrm((DEPTH, HYENA_FFN), 0.02),
        'hyena_freq': 1.0 + nrm((DEPTH, HYENA_FFN), 0.02),
        'hyena_w2': nrm((DEPTH, HYENA_FFN, HYENA_FFN), HYENA_FFN ** -0.5),
        'hyena_b2': nrm((DEPTH, HYENA_FFN), 0.02),
        'hyena_w3': nrm((DEPTH, HYENA_FFN, HYENA_FFN), HYENA_FFN ** -0.5),
        'hyena_b3': nrm((DEPTH, HYENA_FFN), 0.02),
        'hyena_w4': nrm((DEPTH, HYENA_FFN, 2 * C_WIDTH), 0.1 * HYENA_FFN ** -0.5),
        'hyena_b4': nrm((DEPTH, 2 * C_WIDTH), 0.01),
        'hyena_d': nrm((DEPTH, C_WIDTH), 0.5),
        'na_rel_bias': nrm((DEPTH, D_HEADS, 2 * NA_KH_MAX - 1, 2 * NA_KW - 1), 0.5),
        'w_branch': nrm((DEPTH, N_BRANCH, BRANCH_WIDTH, D_MODEL), beta_dn * BRANCH_WIDTH ** -0.5),
        'w_o': nrm((DEPTH, D_MODEL, D_MODEL), beta_dn * D_MODEL ** -0.5),
        'ln1_g': 1.0 + nrm((DEPTH, D_MODEL), 0.02),
        'ln1_b': nrm((DEPTH, D_MODEL), 0.02),
        'ln2_g': 1.0 + nrm((DEPTH, D_MODEL), 0.02),
        'ln2_b': nrm((DEPTH, D_MODEL), 0.02),
        'w_router': nrm((DEPTH, D_MODEL, N_EXPERTS), D_MODEL ** -0.5),
        'w_e1': nrm((DEPTH, N_EXPERTS, D_MODEL, EXPERT_FF), D_MODEL ** -0.5),
        'w_e3': nrm((DEPTH, N_EXPERTS, D_MODEL, EXPERT_FF), D_MODEL ** -0.5),
        'w_e2': nrm((DEPTH, N_EXPERTS, EXPERT_FF, D_MODEL), beta_dn * EXPERT_FF ** -0.5),
    }


def reference(x, c, ctx, c_ctx, w_mod, b_mod, w_in, attn_sink, delta_conv, delta_a_log, delta_dt_bias,
              delta_norm_g, hyena_conv, hyena_w1, hyena_b1, hyena_freq, hyena_w2, hyena_b2, hyena_w3,
              hyena_b3, hyena_w4, hyena_b4, hyena_d, na_rel_bias, w_branch, w_o, ln1_g, ln1_b, ln2_g,
              ln2_b, w_router, w_e1, w_e3, w_e2):
    n = x.shape[1]
    tpos = jnp.arange(n)
    pos_row, pos_col = tpos // GRID_W, tpos % GRID_W
    alpha = (2.0 * DEPTH) ** 0.25
    silu_c = jax.nn.silu(c)
    silu_cc = jax.nn.silu(c_ctx)
    h_c = ctx
    for l in range(DEPTH):
        ctx_out = l < DEPTH - 1
        mod_l = silu_c @ w_mod[l] + b_mod[l]
        mod_c = silu_cc @ w_mod[l] + b_mod[l]
        sh1, sc1, g1, sh2, sc2, g2 = [m[:, None, :] for m in jnp.split(mod_l, 6, axis=-1)]
        csh1, csc1, cg1, csh2, csc2, cg2 = jnp.split(mod_c, 6, axis=-1)
        filt = (hyena_w1[l], hyena_b1[l], hyena_freq[l], hyena_w2[l], hyena_b2[l],
                hyena_w3[l], hyena_b3[l], hyena_w4[l], hyena_b4[l])
        mix_l, mix_c = hybrid_mixer(x * (1.0 + sc1) + sh1, h_c * (1.0 + csc1) + csh1, pos_row, pos_col,
                                    w_in[l], attn_sink[l], delta_conv[l], delta_a_log[l], delta_dt_bias[l],
                                    delta_norm_g[l], hyena_conv[l], filt, hyena_d[l], na_rel_bias[l],
                                    w_branch[l], w_o[l], ctx_out)
        x = layer_norm(alpha * x + g1 * mix_l, ln1_g[l], ln1_b[l])
        ffn_l = expert_choice_ffn(x * (1.0 + sc2) + sh2, w_router[l], w_e1[l], w_e3[l], w_e2[l])
        x = layer_norm(alpha * x + g2 * ffn_l, ln2_g[l], ln2_b[l])
        if ctx_out:
            h_c = layer_norm(alpha * h_c + cg1 * mix_c, ln1_g[l], ln1_b[l])
            ffn_c = expert_choice_ffn(h_c * (1.0 + csc2) + csh2, w_router[l], w_e1[l], w_e3[l], w_e2[l])
            h_c = layer_norm(alpha * h_c + cg2 * ffn_c, ln2_g[l], ln2_b[l])
    return x
```

```python
import functools
import math

import numpy as np
import jax
import jax.numpy as jnp
from jax import lax
from jax.experimental import pallas as pl
from jax.experimental.pallas import tpu as pltpu

F32 = jnp.float32
BF16 = jnp.bfloat16
I32 = jnp.int32
HI = lax.Precision.HIGHEST

D_MODEL = 2048
GRID_W = 64
HEAD_DIM = 128
LANES = 128
BW = D_MODEL // 2
A_HEADS, A_KV = 8, 2
A_WINDOW = 128
A_BLK = 128
B_HEADS = 8
B_CHUNK = 64
D_HEADS = 8
NA_KH, NA_KW = 8, 16
N_EXPERTS = 16
EXPERT_FF = D_MODEL // 2
EC_CAPACITY = 2
HYENA_EMB, HYENA_FFN = 33, 64
ROPE_BASE = 10000.0
LN_EPS = 1e-6
NORM_EPS = 1e-6
NEG_INF = -1e30
DEPTH = 4
ALPHA = (2.0 * DEPTH) ** 0.25

SEG1 = 5632
SEG_AB = 32
SEG3 = 14336
FFT_L = 128

VMEM_LIMIT = 56 * 1024 * 1024


def _cp(sem=None, vmem=VMEM_LIMIT):
    kw = dict(vmem_limit_bytes=vmem)
    if sem is not None:
        kw["dimension_semantics"] = sem
    return pltpu.CompilerParams(**kw)


def _dot(a, b):
    return jnp.dot(a, b, preferred_element_type=F32)


def _dot_hi(a, b):
    return jnp.dot(a, b, preferred_element_type=F32, precision=HI)


def _dot_nt(a, b):
    return lax.dot_general(a, b, (((1,), (1,)), ((), ())), preferred_element_type=F32)


def _dot_nt_hi(a, b):
    return lax.dot_general(a, b, (((1,), (1,)), ((), ())), preferred_element_type=F32, precision=HI)


def _dot_tn(a, b):
    return lax.dot_general(a, b, (((0,), (0,)), ((), ())), preferred_element_type=F32)


def _silu(x):
    return x * jax.nn.sigmoid(x)


def _modvec_kernel(ct_ref, w_ref, b_ref, o_ref):
    s = _silu(ct_ref[...])
    w = w_ref[...]
    r0 = jnp.sum(w * s[:, 0:1], axis=0, keepdims=True) + b_ref[...]
    r1 = jnp.sum(w * s[:, 1:2], axis=0, keepdims=True) + b_ref[...]
    o_ref[...] = jnp.concatenate([r0, r1, jnp.zeros((6, w.shape[1]), F32)], axis=0)


def _modvec(c, c_ctx, w_mod, b_mod):
    depth, d, n6 = w_mod.shape
    tn = 512
    ct = jnp.zeros((d, LANES), F32).at[:, 0].set(c[0]).at[:, 1].set(c_ctx)
    return pl.pallas_call(
        _modvec_kernel,
        grid=(depth, n6 // tn),
        in_specs=[pl.BlockSpec((d, LANES), lambda l, j: (0, 0)),
                  pl.BlockSpec((None, d, tn), lambda l, j: (l, 0, j)),
                  pl.BlockSpec((None, 1, tn), lambda l, j: (l, 0, j))],
        out_specs=pl.BlockSpec((None, 8, tn), lambda l, j: (l, 0, j)),
        out_shape=jax.ShapeDtypeStruct((depth, 8, n6), F32),
        compiler_params=_cp(("parallel", "parallel")),
        name="modvec",
    )(ct, w_mod, b_mod.reshape(depth, 1, n6))


def _pick_mod(m, is_ctx):
    return jnp.where(is_ctx, m[1:2, :], m[0:1, :])


def _modulate_kernel(x_ref, sh_ref, sc_ref, o_ref):
    is_ctx = pl.program_id(0) == pl.num_programs(0) - 1
    sh = _pick_mod(sh_ref[...], is_ctx)
    sc = _pick_mod(sc_ref[...], is_ctx)
    o_ref[...] = (x_ref[...] * (1.0 + sc) + sh).astype(o_ref.dtype)


def _modulate(x, mod_l, nc):
    nt, d = x.shape
    return pl.pallas_call(
        _modulate_kernel,
        grid=(nt // nc,),
        in_specs=[pl.BlockSpec((nc, d), lambda i: (i, 0)),
                  pl.BlockSpec((8, d), lambda i: (0, 0)),
                  pl.BlockSpec((8, d), lambda i: (0, 1))],
        out_specs=pl.BlockSpec((nc, d), lambda i: (i, 0)),
        out_shape=jax.ShapeDtypeStruct((nt, d), BF16),
        compiler_params=_cp(("parallel",)),
        name="modulate",
    )(x, mod_l, mod_l)


def _mm_kernel(a_ref, w_ref, o_ref):
    o_ref[...] = _dot(a_ref[...], w_ref[...].astype(BF16)).astype(o_ref.dtype)


def _matmul(a, w, l, col0, n, tm, tn, out_dtype=F32, name="mm"):
    m, k = a.shape
    assert m % tm == 0 and n % tn == 0 and col0 % tn == 0
    cb = col0 // tn
    return pl.pallas_call(
        _mm_kernel,
        grid=(m // tm, n // tn),
        in_specs=[pl.BlockSpec((tm, k), lambda i, j: (i, 0)),
                  pl.BlockSpec((None, k, tn), lambda i, j: (l, 0, cb + j))],
        out_specs=pl.BlockSpec((tm, tn), lambda i, j: (i, j)),
        out_shape=jax.ShapeDtypeStruct((m, n), out_dtype),
        compiler_params=_cp(("parallel", "parallel")),
        name=name,
    )(a, w)


def _rope_tables(nl, nc):
    half = HEAD_DIM // 2
    nf = half // 2
    inv_freq = ROPE_BASE ** (-jnp.arange(nf, dtype=F32) / nf)
    tpos = jnp.arange(nl)
    ang_r = (tpos // GRID_W).astype(F32)[:, None] * inv_freq[None, :]
    ang_c = (tpos % GRID_W).astype(F32)[:, None] * inv_freq[None, :]
    cr, sr, cc, sc = jnp.cos(ang_r), jnp.sin(ang_r), jnp.cos(ang_c), jnp.sin(ang_c)
    cos_t = jnp.concatenate([cr, cr, cc, cc], axis=1)
    sin_t = jnp.concatenate([-sr, sr, -sc, sc], axis=1)
    cos_t = jnp.concatenate([cos_t, jnp.ones((nc, HEAD_DIM), F32)], axis=0)
    sin_t = jnp.concatenate([sin_t, jnp.zeros((nc, HEAD_DIM), F32)], axis=0)
    return cos_t, sin_t


def _rope_kernel(x_ref, c_ref, s_ref, o_ref, *, n_rot):
    x = x_ref[...]
    lane = lax.broadcasted_iota(I32, x.shape, 1)
    quarter = HEAD_DIM // 4
    swapped = jnp.where((lane % (2 * quarter)) < quarter,
                        pltpu.roll(x, HEAD_DIM - quarter, 1), pltpu.roll(x, quarter, 1))
    y = x * c_ref[...] + swapped * s_ref[...]
    o_ref[...] = jnp.where(pl.program_id(1) < n_rot, y, x).astype(o_ref.dtype)


def _rope(pa, cos_t, sin_t, tr):
    nt = pa.shape[0]
    n_rot = A_HEADS + A_KV
    n_sl = A_HEADS + 2 * A_KV
    return pl.pallas_call(
        functools.partial(_rope_kernel, n_rot=n_rot),
        grid=(nt // tr, n_sl),
        in_specs=[pl.BlockSpec((tr, HEAD_DIM), lambda i, j: (i, j)),
                  pl.BlockSpec((tr, HEAD_DIM), lambda i, j: (i, 0)),
                  pl.BlockSpec((tr, HEAD_DIM), lambda i, j: (i, 0))],
        out_specs=pl.BlockSpec((tr, HEAD_DIM), lambda i, j: (i, j)),
        out_shape=jax.ShapeDtypeStruct((nt, n_sl * HEAD_DIM), BF16),
        compiler_params=_cp(("parallel", "arbitrary")),
        name="rope",
    )(pa, cos_t, sin_t)


def _attn_a_kernel(sink_ref, q_ref, k_ref, v_ref, o_ref, *, nl, nc):
    h = pl.program_id(0)
    qb = pl.program_id(1)
    grp = A_HEADS // A_KV
    scale = HEAD_DIM ** -0.5
    q = q_ref[...]
    qs = jnp.concatenate([q[:, g * HEAD_DIM:(g + 1) * HEAD_DIM] for g in range(grp)], axis=0)
    rows = grp * A_BLK
    wlen = 3 * A_BLK
    is_lat = qb < nl // A_BLK
    ws = pl.multiple_of(jnp.clip((qb - 1) * A_BLK, 0, nl - wlen), A_BLK)
    kc = k_ref[nl:nl + nc, :]
    vc = v_ref[nl:nl + nc, :]
    kw = k_ref[pl.ds(ws, wlen), :]
    vw = v_ref[pl.ds(ws, wlen), :]
    s_c = _dot_nt(qs, kc) * scale
    s_l = _dot_nt(qs, kw) * scale
    ri = lax.broadcasted_iota(I32, (rows, wlen), 0)
    ci = lax.broadcasted_iota(I32, (rows, wlen), 1)
    qabs = qb * A_BLK + (ri % A_BLK)
    kabs = ws + ci
    mask = is_lat & (jnp.abs(kabs - qabs) <= A_WINDOW)
    s_l = jnp.where(mask, s_l, NEG_INF)
    rcol = lax.broadcasted_iota(I32, (rows, 1), 0) // A_BLK
    sink = jnp.zeros((rows, 1), F32)
    for g in range(grp):
        sink = jnp.where(rcol == g, sink_ref[h * grp + g], sink)
    m = jnp.maximum(jnp.maximum(jnp.max(s_c, axis=-1, keepdims=True),
                                jnp.max(s_l, axis=-1, keepdims=True)), sink)
    p_c = jnp.exp(s_c - m)
    p_l = jnp.exp(s_l - m)
    den = (jnp.sum(p_c, axis=-1, keepdims=True) + jnp.sum(p_l, axis=-1, keepdims=True)
           + jnp.exp(sink - m))
    y = (_dot(p_c.astype(BF16), vc) + _dot(p_l.astype(BF16), vw)) / den
    for g in range(grp):
        o_ref[:, g * HEAD_DIM:(g + 1) * HEAD_DIM] = y[g * A_BLK:(g + 1) * A_BLK].astype(o_ref.dtype)


def _attn_a(qkv, sink, nl, nc):
    nt = qkv.shape[0]
    grp = A_HEADS // A_KV
    qw = grp * HEAD_DIM
    kcol = A_HEADS
    vcol = A_HEADS + A_KV
    return pl.pallas_call(
        functools.partial(_attn_a_kernel, nl=nl, nc=nc),
        grid=(A_KV, nt // A_BLK),
        in_specs=[pl.BlockSpec(memory_space=pltpu.SMEM),
                  pl.BlockSpec((A_BLK, qw), lambda h, b: (b, h)),
                  pl.BlockSpec((nt, HEAD_DIM), lambda h, b: (0, kcol + h)),
                  pl.BlockSpec((nt, HEAD_DIM), lambda h, b: (0, vcol + h))],
        out_specs=pl.BlockSpec((A_BLK, qw), lambda h, b: (b, h)),
        out_shape=jax.ShapeDtypeStruct((nt, A_HEADS * HEAD_DIM), BF16),
        compiler_params=_cp(("parallel", "arbitrary")),
        name="attn_a",
    )(sink, qkv, qkv, qkv)


def _na_bias_table(rel_bias):
    cq = jnp.arange(GRID_W)
    col_start = jnp.clip(cq - NA_KW // 2, 0, GRID_W - NA_KW)
    col_ok = (cq[None, :] >= col_start[:, None]) & (cq[None, :] < col_start[:, None] + NA_KW)
    dc = jnp.clip(cq[None, :] - cq[:, None], -(NA_KW - 1), NA_KW - 1) + (NA_KW - 1)
    b = rel_bias.astype(F32)[:, :, dc]
    b = jnp.where(col_ok[None, None], b, NEG_INF)
    rows = [jnp.concatenate([b[:, d0 + j] for j in range(NA_KH)], axis=-1) for d0 in range(NA_KH)]
    return jnp.stack(rows, axis=1)


def _na_kernel(q_ref, k_ref, v_ref, bw_ref, o_ref, *, nl, nc):
    qb = pl.program_id(1)
    scale = HEAD_DIM ** -0.5
    rows = nl // GRID_W
    rpb = nc // GRID_W
    kc = k_ref[nl:nl + nc, :].astype(BF16)
    vc = v_ref[nl:nl + nc, :].astype(BF16)

    @pl.when(qb == nl // nc)
    def _():
        q = q_ref[...].astype(BF16)
        s = _dot_nt(q, kc) * scale
        m = jnp.max(s, axis=-1, keepdims=True)
        p = jnp.exp(s - m)
        den = jnp.sum(p, axis=-1, keepdims=True)
        o_ref[...] = (_dot(p.astype(BF16), vc) / den).astype(o_ref.dtype)

    @pl.when(qb < nl // nc)
    def _():
        for i in range(rpb):
            r = qb * rpb + i
            st = jnp.clip(r - NA_KH // 2, 0, rows - NA_KH)
            q = q_ref[i * GRID_W:(i + 1) * GRID_W, :].astype(BF16)
            w0 = pl.multiple_of(st * GRID_W, GRID_W)
            kw = k_ref[pl.ds(w0, NA_KH * GRID_W), :].astype(BF16)
            vw = v_ref[pl.ds(w0, NA_KH * GRID_W), :].astype(BF16)
            s_c = _dot_nt(q, kc) * scale
            s_l = _dot_nt(q, kw) * scale + bw_ref[st - r + (NA_KH - 1)]
            m = jnp.maximum(jnp.max(s_c, axis=-1, keepdims=True), jnp.max(s_l, axis=-1, keepdims=True))
            p_c = jnp.exp(s_c - m)
            p_l = jnp.exp(s_l - m)
            den = jnp.sum(p_c, axis=-1, keepdims=True) + jnp.sum(p_l, axis=-1, keepdims=True)
            y = (_dot(p_c.astype(BF16), vc) + _dot(p_l.astype(BF16), vw)) / den
            o_ref[i * GRID_W:(i + 1) * GRID_W, :] = y.astype(o_ref.dtype)


def _na(pc, bw, nl, nc, qcol):
    nt = pc.shape[0]
    qc = qcol // HEAD_DIM
    kcb = qc + D_HEADS
    vcb = qc + 2 * D_HEADS
    return pl.pallas_call(
        functools.partial(_na_kernel, nl=nl, nc=nc),
        grid=(D_HEADS, nt // nc),
        in_specs=[pl.BlockSpec((nc, HEAD_DIM), lambda h, b: (b, qc + h)),
                  pl.BlockSpec((nt, HEAD_DIM), lambda h, b: (0, kcb + h)),
                  pl.BlockSpec((nt, HEAD_DIM), lambda h, b: (0, vcb + h)),
                  pl.BlockSpec((None, NA_KH, GRID_W, NA_KH * GRID_W), lambda h, b: (h, 0, 0, 0))],
        out_specs=pl.BlockSpec((nc, HEAD_DIM), lambda h, b: (b, h)),
        out_shape=jax.ShapeDtypeStruct((nt, D_HEADS * HEAD_DIM), BF16),
        compiler_params=_cp(("parallel", "arbitrary")),
        name="na_attn",
    )(pc, pc, pc, bw)


def _conv3(x_ref, p_ref, n_ref, w_ref, scr, nb_lat):
    i = pl.program_id(0)
    r = x_ref.shape[0]
    x = x_ref[...]
    has_prev = (i >= 1) & (i < nb_lat)
    has_next = i < nb_lat - 1
    scr[8:8 + r, :] = x
    scr[7:8, :] = jnp.where(has_prev, p_ref[7:8, :], 0.0)
    scr[8 + r:9 + r, :] = jnp.where(has_next, n_ref[0:1, :], 0.0)
    w = w_ref[...]
    return w[0:1, :] * scr[7:7 + r, :] + w[1:2, :] * x + w[2:3, :] * scr[9:9 + r, :]


def _halo_specs(nc, tc, cb, nt):
    r8 = nc // 8
    last8 = nt // 8 - 1
    return [pl.BlockSpec((nc, tc), lambda i, j: (i, cb + j)),
            pl.BlockSpec((8, tc), lambda i, j: (jnp.maximum(i * r8 - 1, 0), cb + j)),
            pl.BlockSpec((8, tc), lambda i, j: (jnp.minimum((i + 1) * r8, last8), cb + j))]


def _dprep_kernel(x_ref, p_ref, n_ref, w_ref, o_ref, scr, *, nb_lat, n_norm):
    y = _silu(_conv3(x_ref, p_ref, n_ref, w_ref, scr, nb_lat))
    j = pl.program_id(1)
    tc = y.shape[1]
    for s in range(tc // HEAD_DIM):
        ys = y[:, s * HEAD_DIM:(s + 1) * HEAD_DIM]
        nrm = ys * lax.rsqrt(jnp.sum(ys * ys, axis=-1, keepdims=True) + NORM_EPS)
        o_ref[:, s * HEAD_DIM:(s + 1) * HEAD_DIM] = jnp.where(j < n_norm, nrm, ys)


def _dprep(pa, conv_w, nl, nc, col0):
    nt = pa.shape[0]
    tc = 512
    width = 3 * BW
    return pl.pallas_call(
        functools.partial(_dprep_kernel, nb_lat=nl // nc, n_norm=2 * BW // tc),
        grid=(nt // nc, width // tc),
        in_specs=_halo_specs(nc, tc, col0 // tc, nt) + [pl.BlockSpec((3, tc), lambda i, j: (0, j))],
        out_specs=pl.BlockSpec((nc, tc), lambda i, j: (i, j)),
        out_shape=jax.ShapeDtypeStruct((nt, width), F32),
        scratch_shapes=[pltpu.VMEM((nc + 16, tc), F32)],
        compiler_params=_cp(("parallel", "arbitrary")),
        name="delta_prep",
    )(pa, pa, pa, conv_w)


def _softplus(x):
    return jnp.maximum(x, 0.0) + jnp.log1p(jnp.exp(-jnp.abs(x)))


def _delta_kernel(qkv_ref, ab_ref, alog_ref, dtb_ref, o_ref, s_ref):
    d = pl.program_id(0)
    c = pl.program_id(1)
    ck = B_CHUNK
    nh = B_HEADS

    @pl.when(c == 0)
    def _():
        s_ref[...] = jnp.zeros_like(s_ref)

    ab = ab_ref[...]
    g_all = -jnp.exp(alog_ref[...]) * _softplus(ab[:, 0:2 * nh] + dtb_ref[...])
    b_all = jax.nn.sigmoid(ab[:, 2 * nh:4 * nh])
    fwd = d == 0
    g8 = jnp.where(fwd, g_all[:, 0:nh], g_all[:, nh:2 * nh])
    b8 = jnp.where(fwd, b_all[:, 0:nh], b_all[:, nh:2 * nh])
    ri = lax.broadcasted_iota(I32, (ck, ck), 0)
    ci = lax.broadcasted_iota(I32, (ck, ck), 1)
    lag = (ri - ci) * jnp.where(fwd, 1, -1)
    incl = lag >= 0
    strict = lag > 0
    cum8 = _dot_hi(incl.astype(F32), g8)
    tot8 = jnp.sum(g8, axis=0, keepdims=True)
    eye = (ri == ci).astype(F32)
    qkv = qkv_ref
    for h in range(nh):
        sl = slice(h * HEAD_DIM, (h + 1) * HEAD_DIM)
        q = qkv[:, h * HEAD_DIM:(h + 1) * HEAD_DIM] * (HEAD_DIM ** -0.5)
        k = qkv[:, BW + h * HEAD_DIM:BW + (h + 1) * HEAD_DIM]
        v = qkv[:, 2 * BW + h * HEAD_DIM:2 * BW + (h + 1) * HEAD_DIM]
        gc = cum8[:, h:h + 1]
        bc = b8[:, h:h + 1]
        gm = jnp.broadcast_to(gc, (ck, ck))
        decay = jnp.exp(jnp.where(incl, gm - gm.T, NEG_INF))
        kb = k * bc
        vb = v * bc
        kbf, kf = kb.astype(BF16), k.astype(BF16)
        a_mat = jnp.where(strict, _dot_nt(kbf, kf) * decay, 0.0)
        eg = jnp.exp(gc)
        rhs = jnp.concatenate([vb, kb * eg], axis=1)
        inv = eye - a_mat
        pw = _dot_hi(a_mat, a_mat)
        for _ in range(4):
            inv = inv + _dot_hi(inv, pw)
            pw = _dot_hi(pw, pw)
        inv = inv + _dot_hi(inv, pw)
        sol = _dot_hi(inv, rhs)
        u = sol[:, :HEAD_DIM]
        w = sol[:, HEAD_DIM:]
        qk = jnp.where(incl, _dot_nt(q.astype(BF16), kf) * decay, 0.0)
        qg = q * eg
        tot = tot8[:, h:h + 1]
        kd = k * jnp.exp(tot - gc)
        st = s_ref[h]
        stb = st.astype(BF16)
        v_new = u - _dot(w.astype(BF16), stb)
        vnb = v_new.astype(BF16)
        o_ref[:, sl] = _dot(qg.astype(BF16), stb) + _dot(qk.astype(BF16), vnb)
        s_ref[h] = st * jnp.exp(tot) + _dot_tn(kd.astype(BF16), vnb)


def _delta(qkvb, pab, a_log, dt_bias, nl, nc):
    nt = qkvb.shape[0]
    ck = B_CHUNK
    nlc, ncc = nl // ck, nc // ck
    ntc = nlc + ncc

    def rb(d, c):
        f = jnp.where(c < ncc, nlc + c, c - ncc)
        b = jnp.where(c < ncc, nlc + ncc - 1 - c, nlc - 1 - (c - ncc))
        return jnp.where(d == 0, f, b)

    return pl.pallas_call(
        _delta_kernel,
        grid=(2, ntc),
        in_specs=[pl.BlockSpec((ck, 3 * BW), lambda d, c: (rb(d, c), 0)),
                  pl.BlockSpec((ck, 4 * B_HEADS), lambda d, c: (rb(d, c), 0)),
                  pl.BlockSpec((1, 2 * B_HEADS), lambda d, c: (0, 0)),
                  pl.BlockSpec((1, 2 * B_HEADS), lambda d, c: (0, 0))],
        out_specs=pl.BlockSpec((None, ck, BW), lambda d, c: (d, rb(d, c), 0)),
        out_shape=jax.ShapeDtypeStruct((2, nt, BW), F32),
        scratch_shapes=[pltpu.VMEM((B_HEADS, HEAD_DIM, HEAD_DIM), F32)],
        compiler_params=_cp(("arbitrary", "arbitrary")),
        name="delta_scan",
    )(qkvb, pab, a_log.reshape(1, 2 * B_HEADS), dt_bias.reshape(1, 2 * B_HEADS))


def _dpost_kernel(of_ref, ob_ref, g_ref, ng_ref, o_ref):
    o = of_ref[...] + ob_ref[...]
    gate = g_ref[...]
    ng = ng_ref[...]
    for s in range(o.shape[1] // HEAD_DIM):
        sl = slice(s * HEAD_DIM, (s + 1) * HEAD_DIM)
        os_ = o[:, sl]
        y = os_ * lax.rsqrt(jnp.mean(os_ * os_, axis=-1, keepdims=True) + NORM_EPS) * ng
        o_ref[:, sl] = (y * _silu(gate[:, sl])).astype(o_ref.dtype)


def _dpost(o2, pa, norm_g, nc, gcol):
    nt = pa.shape[0]
    tc = 512
    gb = gcol // tc
    return pl.pallas_call(
        _dpost_kernel,
        grid=(nt // nc, BW // tc),
        in_specs=[pl.BlockSpec((None, nc, tc), lambda i, j: (0, i, j)),
                  pl.BlockSpec((None, nc, tc), lambda i, j: (1, i, j)),
                  pl.BlockSpec((nc, tc), lambda i, j: (i, gb + j)),
                  pl.BlockSpec((1, HEAD_DIM), lambda i, j: (0, 0))],
        out_specs=pl.BlockSpec((nc, tc), lambda i, j: (i, j)),
        out_shape=jax.ShapeDtypeStruct((nt, BW), BF16),
        compiler_params=_cp(("parallel", "parallel")),
        name="delta_post",
    )(o2, o2, pa, norm_g.reshape(1, HEAD_DIM))


def _hconv_kernel(a_ref, ap_ref, an_ref, b_ref, bp_ref, bn_ref, c_ref, cp_ref, cn_ref,
                  wa_ref, wb_ref, wc_ref, x0_ref, z_ref, scr, *, nb_lat):
    x0_ref[...] = _conv3(a_ref, ap_ref, an_ref, wa_ref, scr, nb_lat)
    x1 = _conv3(b_ref, bp_ref, bn_ref, wb_ref, scr, nb_lat)
    v = _conv3(c_ref, cp_ref, cn_ref, wc_ref, scr, nb_lat)
    z_ref[...] = v * x1


def _hconv(pc, conv_w, nl, nc):
    nt = pc.shape[0]
    tc = 512
    gpb = BW // tc
    specs = []
    for g in range(3):
        specs += _halo_specs(nc, tc, g * gpb, nt)
    for g in range(3):
        specs.append(pl.BlockSpec((3, tc), lambda i, j, g=g: (0, g * gpb + j)))
    out = pl.BlockSpec((nc, tc), lambda i, j: (i, j))
    return pl.pallas_call(
        functools.partial(_hconv_kernel, nb_lat=nl // nc),
        grid=(nt // nc, gpb),
        in_specs=specs,
        out_specs=[out, out],
        out_shape=[jax.ShapeDtypeStruct((nt, BW), F32)] * 2,
        scratch_shapes=[pltpu.VMEM((nc + 16, tc), F32)],
        compiler_params=_cp(("parallel", "arbitrary")),
        name="hyena_conv",
    )(pc, pc, pc, pc, pc, pc, pc, pc, pc, conv_w, conv_w, conv_w)


def _hfilter_features(n):
    t = np.linspace(0.0, 1.0, n, dtype=np.float32)[:, None]
    bands = (HYENA_EMB - 1) // 2
    wpos = (2.0 * math.pi * np.arange(n, dtype=np.float32)[:, None] / n).astype(np.float32)
    f = np.linspace(1e-4, bands - 1, bands, dtype=np.float32)[None, :]
    z = np.concatenate([t, np.cos(f * wpos), -np.sin(f * wpos)], axis=-1).astype(np.float32)
    pos = np.concatenate([np.arange(n), [0], 2 * n - np.arange(n + 1, 2 * n)])
    zk = np.zeros((2 * n, LANES), np.float32)
    zk[:, :HYENA_EMB] = z[pos]
    max_decay = math.log(1e-2) / 0.3
    min_decay = math.log(1e-2) / 1.5
    absd = np.abs(np.linspace(min_decay, max_decay, BW, dtype=np.float32))[None, :]
    return jnp.asarray(zk), jnp.asarray(absd)


def _hfilter_kernel(z_ref, w1, b1, fr, w2, b2, w3, b3, w4, b4, ad, o_ref, *, n, tr):
    z = z_ref[...]
    frq = fr[...]
    h = jnp.sin(frq * (_dot_hi(z, w1[...]) + b1[...]))
    h = jnp.sin(frq * (_dot_hi(h, w2[...]) + b2[...]))
    h = jnp.sin(frq * (_dot_hi(h, w3[...]) + b3[...]))
    h = _dot_hi(h, w4[...]) + b4[...]
    win = jnp.exp(-z[:, 0:1] * ad[...])
    row = pl.program_id(0) * tr + lax.broadcasted_iota(I32, (tr, 1), 0)
    o_ref[...] = jnp.where(row == n, 0.0, h * win)


def _hfilter(n, zk, absd, w1, b1, fr, w2, b2, w3, b3, w4, b4):
    tr = min(512, n)
    nfb = n // tr
    w1p = jnp.zeros((LANES, HYENA_FFN), F32).at[:HYENA_EMB].set(w1)
    small = lambda shp: pl.BlockSpec(shp, lambda i: (0, 0))
    row = lambda a: a.reshape(1, -1)
    return pl.pallas_call(
        functools.partial(_hfilter_kernel, n=n, tr=tr),
        grid=(2 * n // tr,),
        in_specs=[pl.BlockSpec((tr, LANES), lambda i: (i, 0)),
                  small((LANES, HYENA_FFN)), small((1, HYENA_FFN)), small((1, HYENA_FFN)),
                  small((HYENA_FFN, HYENA_FFN)), small((1, HYENA_FFN)),
                  small((HYENA_FFN, HYENA_FFN)), small((1, HYENA_FFN)),
                  pl.BlockSpec((HYENA_FFN, BW), lambda i: (0, (i >= nfb).astype(I32))),
                  pl.BlockSpec((1, BW), lambda i: (0, (i >= nfb).astype(I32))),
                  small((1, BW))],
        out_specs=pl.BlockSpec((tr, BW), lambda i: (i, 0)),
        out_shape=jax.ShapeDtypeStruct((2 * n, BW), F32),
        compiler_params=_cp(("parallel",)),
        name="hyena_filter",
    )(zk, w1p, row(b1), row(fr), w2, row(b2), w3, row(b3), w4, row(b4), absd)


def _fft_tables(n):
    big = 2 * n
    h = big // FFT_L
    l2 = FFT_L // 2

    def cs(a, b, period):
        ang = 2.0 * np.pi * np.outer(np.arange(a), np.arange(b)) / period
        return np.cos(ang), np.sin(ang)

    ch, sh = cs(h, h, h)
    cl, sl = cs(FFT_L, FFT_L, FFT_L)
    ct, st = cs(h, FFT_L, big)
    f = lambda a: jnp.asarray(a, F32)
    return dict(
        fhr=f(ch), fhi=f(-sh),
        flr=f(cl[:l2]), fli=f(-sl[:l2]),
        twr=f(ct), twi=f(-st),
        clr=f(cl[:, :l2]), cli=f(sl[:, :l2]),
        chr=f(ch[:h // 2]), chi=f(sh[:h // 2]),
        ctr=f(ct.T), cti=f(st.T),
    )


def _alt_sum(x):
    row = lax.broadcasted_iota(I32, x.shape, 0)
    return jnp.sum(jnp.where((row & 1) == 1, -x, x), axis=0, keepdims=True)


def _fft_stage1(x_ref, vr_ref, vi_ref, fhr_ref, fhi_ref, h_in, h):
    fr = fhr_ref[:, 0:h_in].astype(BF16)
    fi = fhi_ref[:, 0:h_in].astype(BF16)

    def body(m2, carry):
        a = x_ref[pl.ds(m2, h_in, stride=FFT_L), :].astype(BF16)
        vr_ref[pl.ds(m2, h, stride=FFT_L), :] = _dot(fr, a)
        vi_ref[pl.ds(m2, h, stride=FFT_L), :] = _dot(fi, a)
        return carry

    lax.fori_loop(0, FFT_L, body, 0)


def _fft_stage2_block(f1, vr_ref, vi_ref, flr_ref, fli_ref, twr_ref, twi_ref):
    twr = twr_ref[pl.ds(f1, 1), :]
    twi = twi_ref[pl.ds(f1, 1), :]
    flr = flr_ref[...]
    fli = fli_ref[...]
    gr = (flr * twr - fli * twi).astype(BF16)
    gi = (flr * twi + fli * twr).astype(BF16)
    r0 = pl.multiple_of(f1 * FFT_L, FFT_L)
    vr = vr_ref[pl.ds(r0, FFT_L), :].astype(BF16)
    vi = vi_ref[pl.ds(r0, FFT_L), :].astype(BF16)
    xr = _dot(gr, vr) - _dot(gi, vi)
    xi = _dot(gr, vi) + _dot(gi, vr)
    return xr, xi


def _kfft_kernel(k_ref, fhr, fhi, flr, fli, twr, twi, kr_ref, ki_ref, kn_ref, vr_ref, vi_ref, *, h):
    l2 = FFT_L // 2
    kn_ref[...] = _alt_sum(k_ref[...])
    _fft_stage1(k_ref, vr_ref, vi_ref, fhr, fhi, h, h)

    def body(f1, carry):
        xr, xi = _fft_stage2_block(f1, vr_ref, vi_ref, flr, fli, twr, twi)
        k0 = pl.multiple_of(f1 * l2, l2)
        kr_ref[pl.ds(k0, l2), :] = xr
        ki_ref[pl.ds(k0, l2), :] = xi
        return carry

    lax.fori_loop(0, h, body, 0)


def _table_specs(tabs, names):
    return [pl.BlockSpec(tabs[k].shape, lambda j: (0, 0)) for k in names]


def _kfft(kt, tabs, n):
    big = 2 * n
    h = big // FFT_L
    l2 = FFT_L // 2
    names = ["fhr", "fhi", "flr", "fli", "twr", "twi"]
    spec = pl.BlockSpec((h * l2, LANES), lambda j: (0, j))
    return pl.pallas_call(
        functools.partial(_kfft_kernel, h=h),
        grid=(BW // LANES,),
        in_specs=[pl.BlockSpec((big, LANES), lambda j: (0, j))] + _table_specs(tabs, names),
        out_specs=[spec, spec, pl.BlockSpec((1, LANES), lambda j: (0, j))],
        out_shape=[jax.ShapeDtypeStruct((h * l2, BW), F32)] * 2 + [jax.ShapeDtypeStruct((1, BW), F32)],
        scratch_shapes=[pltpu.VMEM((big, LANES), F32)] * 2,
        compiler_params=_cp(("parallel",)),
        name="hyena_filter_fft",
    )(kt, *[tabs[k] for k in names])


def _zfft_kernel(z_ref, x0_ref, kr_ref, ki_ref, kn_ref, d_ref, fhr, fhi, flr, fli, twr, twi,
                 clr, cli, chr_, chi, ctr, cti, o_ref, vr_ref, vi_ref, *, h):
    l2 = FFT_L // 2
    hh = h // 2
    big = h * FFT_L
    pn = _alt_sum(z_ref[...]) * kn_ref[...] * (1.0 / big)
    _fft_stage1(z_ref, vr_ref, vi_ref, fhr, fhi, hh, h)
    cr = clr[...].astype(BF16)
    ci = cli[...].astype(BF16)
    rowi = lax.broadcasted_iota(I32, (l2, 1), 0)

    def mid(f1, carry):
        xr, xi = _fft_stage2_block(f1, vr_ref, vi_ref, flr, fli, twr, twi)
        k0 = pl.multiple_of(f1 * l2, l2)
        kr = kr_ref[pl.ds(k0, l2), :]
        ki = ki_ref[pl.ds(k0, l2), :]
        half_dc = jnp.where((rowi == 0) & (f1 == 0), 0.5, 1.0)
        pr = ((xr * kr - xi * ki) * half_dc).astype(BF16)
        pi = ((xr * ki + xi * kr) * half_dc).astype(BF16)
        r0 = pl.multiple_of(f1 * FFT_L, FFT_L)
        vr_ref[pl.ds(r0, FFT_L), :] = _dot(cr, pr) - _dot(ci, pi)
        vi_ref[pl.ds(r0, FFT_L), :] = _dot(cr, pi) + _dot(ci, pr)
        return carry

    lax.fori_loop(0, h, mid, 0)
    ch_r = chr_[...]
    ch_i = chi[...]
    dvec = d_ref[...]

    def last(m2, carry):
        tr_ = ctr[pl.ds(m2, 1), :]
        ti_ = cti[pl.ds(m2, 1), :]
        gr = (ch_r * tr_ - ch_i * ti_).astype(BF16)
        gi = (ch_r * ti_ + ch_i * tr_).astype(BF16)
        zr = vr_ref[pl.ds(m2, h, stride=FFT_L), :].astype(BF16)
        zi = vi_ref[pl.ds(m2, h, stride=FFT_L), :].astype(BF16)
        sgn = jnp.where((m2 & 1) == 1, -1.0, 1.0)
        y = (_dot(gr, zr) - _dot(gi, zi)) * (2.0 / big) + pn * sgn
        zt = z_ref[pl.ds(m2, hh, stride=FFT_L), :]
        x0 = x0_ref[pl.ds(m2, hh, stride=FFT_L), :]
        o_ref[pl.ds(m2, hh, stride=FFT_L), :] = (y + dvec * zt) * x0
        return carry

    lax.fori_loop(0, FFT_L, last, 0)


def _zfft(z, x0c, kr, ki, kn, d_bias, tabs, nl):
    big = 2 * nl
    h = big // FFT_L
    l2 = FFT_L // 2
    names = ["fhr", "fhi", "flr", "fli", "twr", "twi", "clr", "cli", "chr", "chi", "ctr", "cti"]
    one = pl.Buffered(1)
    col = lambda rows: pl.BlockSpec((rows, LANES), lambda j: (0, j), pipeline_mode=one)
    vec = pl.BlockSpec((1, LANES), lambda j: (0, j))
    return pl.pallas_call(
        functools.partial(_zfft_kernel, h=h),
        grid=(BW // LANES,),
        in_specs=[col(nl), col(nl), col(h * l2), col(h * l2), vec, vec] + _table_specs(tabs, names),
        out_specs=pl.BlockSpec((nl, LANES), lambda j: (0, j)),
        out_shape=jax.ShapeDtypeStruct((nl, BW), F32),
        scratch_shapes=[pltpu.VMEM((big, LANES), F32)] * 2,
        compiler_params=_cp(("parallel",)),
        name="hyena_fftconv",
    )(z, x0c, kr, ki, kn, d_bias.reshape(1, BW), *[tabs[k] for k in names])


def _hsmall_kernel(z_ref, x0_ref, k_ref, d_ref, fr_ref, fi_ref, o_ref, *, nc):
    big = 2 * nc
    fr = fr_ref[...].astype(BF16)
    fi = fi_ref[...].astype(BF16)
    z = z_ref[...]
    zb = z.astype(BF16)
    kb = k_ref[...].astype(BF16)
    xr = _dot(fr[:, :nc], zb)
    xi = _dot(fi[:, :nc], zb)
    kr = _dot(fr, kb)
    ki = _dot(fi, kb)
    pr = (xr * kr - xi * ki).astype(BF16)
    pi = (xr * ki + xi * kr).astype(BF16)
    y = (_dot(fr[:nc, :], pr) + _dot(fi[:nc, :], pi)) * (1.0 / big)
    o_ref[...] = (y + d_ref[...] * z) * x0_ref[...]


def _hsmall(z, x0c, kt_c, d_bias, nl, nc):
    big = 2 * nc
    ang = 2.0 * np.pi * np.outer(np.arange(big), np.arange(big)) / big
    fr = jnp.asarray(np.cos(ang), F32)
    fi = jnp.asarray(-np.sin(ang), F32)
    tc = 256
    rb = nl // nc
    return pl.pallas_call(
        functools.partial(_hsmall_kernel, nc=nc),
        grid=(BW // tc,),
        in_specs=[pl.BlockSpec((nc, tc), lambda j: (rb, j)),
                  pl.BlockSpec((nc, tc), lambda j: (rb, j)),
                  pl.BlockSpec((big, tc), lambda j: (0, j)),
                  pl.BlockSpec((1, tc), lambda j: (0, j)),
                  pl.BlockSpec((big, big), lambda j: (0, 0)),
                  pl.BlockSpec((big, big), lambda j: (0, 0))],
        out_specs=pl.BlockSpec((nc, tc), lambda j: (0, j)),
        out_shape=jax.ShapeDtypeStruct((nc, BW), F32),
        compiler_params=_cp(("parallel",)),
        name="hyena_ctx",
    )(z, x0c, kt_c, d_bias.reshape(1, BW), fr, fi)


def _merge_kernel(ya, yb, ycl, ycc, yd, g0, g1, g2, g3, wb_ref, o_ref):
    is_ctx = pl.program_id(1) == pl.num_programs(1) - 1
    yc = jnp.where(is_ctx, ycc[...], ycl[...])
    acc = None
    for b, (y, g) in enumerate(((ya[...], g0), (yb[...], g1), (yc, g2), (yd[...], g3))):
        p = _dot(y.astype(BF16), wb_ref[b].astype(BF16))
        t = jax.nn.sigmoid(g[...]) * p
        acc = t if acc is None else acc + t
    o_ref[...] = acc.astype(o_ref.dtype)


def _merge(ya, yb, yc_lat, yc_ctx, yd, pc, w_branch, l, gcol, nc):
    nt = ya.shape[0]
    tn = 512
    d = D_MODEL
    nb_lat = yc_lat.shape[0] // nc
    ysp = pl.BlockSpec((nc, BW), lambda j, i: (i, 0))
    gsp = [pl.BlockSpec((nc, tn), lambda j, i, b=b: (i, (gcol + b * d) // tn + j)) for b in range(4)]
    return pl.pallas_call(
        _merge_kernel,
        grid=(d // tn, nt // nc),
        in_specs=[ysp, ysp,
                  pl.BlockSpec((nc, BW), lambda j, i: (jnp.minimum(i, nb_lat - 1), 0)),
                  pl.BlockSpec((nc, BW), lambda j, i: (0, 0)),
                  ysp] + gsp + [pl.BlockSpec((None, 4, BW, tn), lambda j, i: (l, 0, 0, j))],
        out_specs=pl.BlockSpec((nc, tn), lambda j, i: (i, j)),
        out_shape=jax.ShapeDtypeStruct((nt, d), BF16),
        compiler_params=_cp(("parallel", "arbitrary")),
        name="merge",
    )(ya, yb, yc_lat, yc_ctx, yd, pc, pc, pc, pc, w_branch)


def _post_norm(x, delta, gate, lng, lnb):
    y = ALPHA * x + gate * delta
    mu = jnp.mean(y, axis=-1, keepdims=True)
    yc = y - mu
    var = jnp.mean(yc * yc, axis=-1, keepdims=True)
    return yc * lax.rsqrt(var + LN_EPS) * lng + lnb


def _ln1_kernel(x_ref, dl_ref, g_ref, lng_ref, lnb_ref, sh_ref, sc_ref, wr_ref, xo_ref, uo_ref, lg_ref):
    is_ctx = pl.program_id(0) == pl.num_programs(0) - 1
    xn = _post_norm(x_ref[...], dl_ref[...], _pick_mod(g_ref[...], is_ctx), lng_ref[...], lnb_ref[...])
    xo_ref[...] = xn
    u = xn * (1.0 + _pick_mod(sc_ref[...], is_ctx)) + _pick_mod(sh_ref[...], is_ctx)
    uo_ref[...] = u
    lg_ref[...] = _dot_nt_hi(wr_ref[...], u)


def _ln1(x, mix, mod_l, lng, lnb, w_router, nc):
    nt, d = x.shape
    row = pl.BlockSpec((nc, d), lambda i: (i, 0))
    modc = lambda c: pl.BlockSpec((8, d), lambda i: (0, c))
    vec = pl.BlockSpec((1, d), lambda i: (0, 0))
    return pl.pallas_call(
        _ln1_kernel,
        grid=(nt // nc,),
        in_specs=[row, row, modc(2), vec, vec, modc(3), modc(4),
                  pl.BlockSpec((N_EXPERTS, d), lambda i: (0, 0))],
        out_specs=[row, row, pl.BlockSpec((N_EXPERTS, nc), lambda i: (0, i))],
        out_shape=[jax.ShapeDtypeStruct((nt, d), F32), jax.ShapeDtypeStruct((nt, d), F32),
                   jax.ShapeDtypeStruct((N_EXPERTS, nt), F32)],
        compiler_params=_cp(("parallel",)),
        name="ln1_router",
    )(x, mix, mod_l, lng.reshape(1, d), lnb.reshape(1, d), mod_l, mod_l, w_router.T)


def _cumsum_lanes(x):
    n = x.shape[1]
    lane = lax.broadcasted_iota(I32, x.shape, 1)
    s = 1
    while s < n:
        x = x + jnp.where(lane >= s, pltpu.roll(x, s, 1), 0.0)
        s *= 2
    return x


def _select_kernel(lg_ref, idx_ref, w_ref, posx_ref, posm_ref, aff_scr, pos_scr, *, n, cap, sc):
    lg = lg_ref[...]
    e = jnp.exp(lg - jnp.max(lg, axis=0, keepdims=True))
    aff = e / jnp.sum(e, axis=0, keepdims=True)
    bits = pltpu.bitcast(aff, I32)
    tau = jnp.zeros((N_EXPERTS, 1), I32)
    for bit in range(30, -1, -1):
        cand = tau | (1 << bit)
        cnt = jnp.sum((bits >= cand).astype(F32), axis=1, keepdims=True)
        tau = jnp.where(cnt >= cap, cand, tau)
    gt = bits > tau
    eq = (bits == tau).astype(F32)
    need = cap - jnp.sum(gt.astype(F32), axis=1, keepdims=True)
    eq_rank = _cumsum_lanes(eq) - eq
    sel = jnp.where(gt, 1.0, jnp.where(eq_rank < need, eq, 0.0))
    posx = _cumsum_lanes(sel) - sel
    posx_ref[...] = posx.astype(I32)
    posm = jnp.where(sel > 0.0, posx, -1.0)
    posm_ref[...] = posm.astype(I32)
    aff_scr[...] = aff
    pos_scr[...] = posm
    nch = cap // sc
    tok = lax.broadcasted_iota(I32, (sc, n), 1).astype(F32)
    slot0 = lax.broadcasted_iota(I32, (sc, 1), 0).astype(F32)
    lane = lax.broadcasted_iota(I32, (sc, LANES), 1)

    def per_expert(ex, carry):
        it, wt = carry
        prow = pos_scr[pl.ds(ex, 1), :]
        arow = aff_scr[pl.ds(ex, 1), :]
        for ch in range(nch):
            hit = prow == (slot0 + float(ch * sc))
            icol = jnp.sum(jnp.where(hit, tok, 0.0), axis=1, keepdims=True)
            wcol = jnp.sum(jnp.where(hit, arow, 0.0), axis=1, keepdims=True)
            here = lane == ex * nch + ch
            it = jnp.where(here, icol, it)
            wt = jnp.where(here, wcol, wt)
        return it, wt

    zero = jnp.zeros((sc, LANES), F32)
    it, wt = lax.fori_loop(0, N_EXPERTS, per_expert, (zero, zero))
    idx_ref[...] = it.astype(I32)
    w_ref[...] = wt


def _select(lg_t, col0, n, cap):
    sc = min(LANES, cap)
    nch = cap // sc
    assert nch * N_EXPERTS <= LANES and col0 % n == 0
    idx_t, w_t, posx, posm = pl.pallas_call(
        functools.partial(_select_kernel, n=n, cap=cap, sc=sc),
        grid=(1,),
        in_specs=[pl.BlockSpec((N_EXPERTS, n), lambda i: (0, col0 // n))],
        out_specs=[pl.BlockSpec((sc, LANES), lambda i: (0, 0)), pl.BlockSpec((sc, LANES), lambda i: (0, 0)),
                   pl.BlockSpec((N_EXPERTS, n), lambda i: (0, 0)), pl.BlockSpec((N_EXPERTS, n), lambda i: (0, 0))],
        out_shape=[jax.ShapeDtypeStruct((sc, LANES), I32), jax.ShapeDtypeStruct((sc, LANES), F32),
                   jax.ShapeDtypeStruct((N_EXPERTS, n), I32), jax.ShapeDtypeStruct((N_EXPERTS, n), I32)],
        scratch_shapes=[pltpu.VMEM((N_EXPERTS, n), F32)] * 2,
        compiler_params=_cp(("arbitrary",)),
        name="moe_select",
    )(lg_t)
    unt = lambda a: a[:, :N_EXPERTS * nch].T.reshape(N_EXPERTS, cap)
    return unt(idx_t), unt(w_t), posx, posm


def _ffn_kernel(idx_ref, u_hbm, ws_ref, w1_ref, w3_ref, w2_ref, o_ref, xg, xb, acc, sem, *, capt, capp):
    ex = pl.program_id(0)
    f = pl.program_id(1)

    def row_copy(r, src_row):
        return pltpu.make_async_copy(u_hbm.at[pl.ds(src_row, 1)], xg.at[pl.ds(r, 1)], sem)

    @pl.when(f == 0)
    def _():
        def issue(r, carry):
            row_copy(r, idx_ref[ex, r]).start()
            return carry

        lax.fori_loop(0, capt, issue, 0)

        def drain(r, carry):
            row_copy(r, 0).wait()
            return carry

        lax.fori_loop(0, capt, drain, 0)
        xb[...] = xg[...].astype(BF16)
        acc[...] = jnp.zeros_like(acc)

    x = xb[...]
    hdn = _silu(_dot(x, w1_ref[...].astype(BF16))) * _dot(x, w3_ref[...].astype(BF16))
    acc[...] += _dot(hdn.astype(BF16), w2_ref[...].astype(BF16))

    @pl.when(f == pl.num_programs(1) - 1)
    def _():
        o_ref[0:capt, :] = (acc[...] * ws_ref[...]).astype(o_ref.dtype)
        o_ref[capt:capp, :] = jnp.zeros((capp - capt, o_ref.shape[1]), o_ref.dtype)


def _ffn(idx, u2, wsel, w_e1, w_e3, w_e2, l, capp):
    capt = idx.shape[1]
    d = D_MODEL
    tf = 256
    grid_spec = pltpu.PrefetchScalarGridSpec(
        num_scalar_prefetch=1,
        grid=(N_EXPERTS, EXPERT_FF // tf),
        in_specs=[pl.BlockSpec(memory_space=pl.ANY),
                  pl.BlockSpec((None, capt, 1), lambda e, f, idx: (e, 0, 0)),
                  pl.BlockSpec((None, None, d, tf), lambda e, f, idx: (l, e, 0, f)),
                  pl.BlockSpec((None, None, d, tf), lambda e, f, idx: (l, e, 0, f)),
                  pl.BlockSpec((None, None, tf, d), lambda e, f, idx: (l, e, f, 0))],
        out_specs=pl.BlockSpec((None, capp, d), lambda e, f, idx: (e, 0, 0)),
        scratch_shapes=[pltpu.VMEM((capt, d), F32), pltpu.VMEM((capt, d), BF16), pltpu.VMEM((capt, d), F32),
                        pltpu.SemaphoreType.DMA(())],
    )
    return pl.pallas_call(
        functools.partial(_ffn_kernel, capt=capt, capp=capp),
        grid_spec=grid_spec,
        out_shape=jax.ShapeDtypeStruct((N_EXPERTS, capp, d), BF16),
        compiler_params=_cp(("arbitrary", "arbitrary")),
        name="moe_ffn",
    )(idx, u2, wsel[..., None], w_e1, w_e3, w_e2)


def _combine_kernel(st_ref, posm_ref, y_hbm, x_ref, g_ref, lng_ref, lnb_ref, sh_ref, sc_ref,
                    xo_ref, uo_ref, ywin, sem, *, win):
    b = pl.program_id(0)
    is_ctx = b == pl.num_programs(0) - 1
    nc = x_ref.shape[0]
    bases, copies = [], []
    for ex in range(N_EXPERTS):
        s0 = st_ref[ex, b]
        base = pl.multiple_of(s0 - (s0 & 15), 16)
        cp = pltpu.make_async_copy(y_hbm.at[ex, pl.ds(base, win)], ywin.at[ex], sem.at[ex])
        cp.start()
        bases.append(base)
        copies.append(cp)
    posm = posm_ref[...]
    slot = lax.broadcasted_iota(I32, (win, nc), 0)
    acc = jnp.zeros((nc, x_ref.shape[1]), F32)
    for ex in range(N_EXPERTS):
        copies[ex].wait()
        hit = (posm[ex:ex + 1, :] - bases[ex]) == slot
        acc = acc + _dot_tn(hit.astype(BF16), ywin[ex])
    xn = _post_norm(x_ref[...], acc, _pick_mod(g_ref[...], is_ctx), lng_ref[...], lnb_ref[...])
    xo_ref[...] = xn
    u = xn * (1.0 + _pick_mod(sc_ref[...], is_ctx)) + _pick_mod(sh_ref[...], is_ctx)
    uo_ref[...] = u.astype(uo_ref.dtype)


def _combine(starts, posm, y, x1, mod_l, lng, lnb, mod_next, nc):
    nt, d = x1.shape
    win = nc + 16
    row = lambda: pl.BlockSpec((nc, d), lambda i, st: (i, 0))
    modc = lambda c: pl.BlockSpec((8, d), lambda i, st: (0, c))
    vec = pl.BlockSpec((1, d), lambda i, st: (0, 0))
    grid_spec = pltpu.PrefetchScalarGridSpec(
        num_scalar_prefetch=1,
        grid=(nt // nc,),
        in_specs=[pl.BlockSpec((N_EXPERTS, nc), lambda i, st: (0, i)),
                  pl.BlockSpec(memory_space=pl.ANY),
                  row(), modc(5), vec, vec, modc(0), modc(1)],
        out_specs=[row(), row()],
        scratch_shapes=[pltpu.VMEM((N_EXPERTS, win, d), BF16), pltpu.SemaphoreType.DMA((N_EXPERTS,))],
    )
    return pl.pallas_call(
        functools.partial(_combine_kernel, win=win),
        grid_spec=grid_spec,
        out_shape=[jax.ShapeDtypeStruct((nt, d), F32), jax.ShapeDtypeStruct((nt, d), BF16)],
        compiler_params=_cp(("arbitrary",)),
        name="moe_combine_ln2",
    )(starts, posm, y, x1, mod_l, lng.reshape(1, d), lnb.reshape(1, d), mod_next, mod_next)


def _row_tile(nt, nc):
    return nt // 4 if (nt % 4 == 0 and (nt // 4) % 16 == 0 and nt // 4 >= nc) else nc


def kernel(x, c, ctx, c_ctx, w_mod, b_mod, w_in, attn_sink, delta_conv, delta_a_log, delta_dt_bias, delta_norm_g, hyena_conv, hyena_w1, hyena_b1, hyena_freq, hyena_w2, hyena_b2, hyena_w3, hyena_b3, hyena_w4, hyena_b4, hyena_d, na_rel_bias, w_branch, w_o, ln1_g, ln1_b, ln2_g, ln2_b, w_router, w_e1, w_e3, w_e2):
    nl, nc = x.shape[1], ctx.shape[1]
    nt = nl + nc
    depth = w_in.shape[0]
    tm = _row_tile(nt, nc)
    cap_l = EC_CAPACITY * nl // N_EXPERTS
    cap_c = EC_CAPACITY * nc // N_EXPERTS
    capp = cap_l + cap_c + nc + 16

    xc = jnp.concatenate([x[0], ctx[0]], axis=0)
    mod = _modvec(c, c_ctx, w_mod, b_mod)
    cos_t, sin_t = _rope_tables(nl, nc)
    tabs = _fft_tables(nl)
    zk_l, absd = _hfilter_features(nl)
    zk_c, _ = _hfilter_features(nc)
    u = _modulate(xc, mod[0], nc)

    for l in range(depth):
        pa = _matmul(u, w_in, l, 0, SEG1, tm, 512, name="in_proj_a")
        pab = _matmul(u, w_in[l, :, SEG1:SEG1 + SEG_AB][None], 0, 0, SEG_AB, tm, SEG_AB, name="in_proj_ab")
        pc = _matmul(u, w_in[l, :, SEG1 + SEG_AB:][None], 0, 0, SEG3, tm, 512, name="in_proj_c")
        ya = _attn_a(_rope(pa, cos_t, sin_t, nc), attn_sink[l], nl, nc)
        qkvb = _dprep(pa, delta_conv[l], nl, nc, A_HEADS * HEAD_DIM + 2 * A_KV * HEAD_DIM)
        o2 = _delta(qkvb, pab, delta_a_log[l], delta_dt_bias[l], nl, nc)
        yb = _dpost(o2, pa, delta_norm_g[l], nc, SEG1 - BW)
        filt = (hyena_w1[l], hyena_b1[l], hyena_freq[l], hyena_w2[l], hyena_b2[l], hyena_w3[l], hyena_b3[l],
                hyena_w4[l], hyena_b4[l])
        x0c, z = _hconv(pc, hyena_conv[l], nl, nc)
        kr, ki, kn = _kfft(_hfilter(nl, zk_l, absd, *filt), tabs, nl)
        yc_lat = _zfft(z, x0c, kr, ki, kn, hyena_d[l], tabs, nl)
        yc_ctx = _hsmall(z, x0c, _hfilter(nc, zk_c, absd, *filt), hyena_d[l], nl, nc)
        yd = _na(pc, _na_bias_table(na_rel_bias[l]), nl, nc, 3 * BW)
        mrg = _merge(ya, yb, yc_lat, yc_ctx, yd, pc, w_branch, l, 6 * BW, nc)
        mix = _matmul(mrg, w_o, l, 0, D_MODEL, tm, 512, name="out_proj")
        x1, u2, lg_t = _ln1(xc, mix, mod[l], ln1_g[l], ln1_b[l], w_router[l], nc)
        idx_l, ws_l, posx_l, posm_l = _select(lg_t, 0, nl, cap_l)
        idx_c, ws_c, posx_c, posm_c = _select(lg_t, nl, nc, cap_c)
        idx = jnp.concatenate([idx_l, idx_c + nl], axis=1)
        wsel = jnp.concatenate([ws_l, ws_c], axis=1)
        y = _ffn(idx, u2, wsel, w_e1, w_e3, w_e2, l, capp)
        posm = jnp.concatenate([posm_l, jnp.where(posm_c >= 0, posm_c + cap_l, -1)], axis=1)
        starts = jnp.concatenate([posx_l[:, ::nc], posx_c[:, :1] + cap_l], axis=1)
        xc, u = _combine(starts, posm, y, x1, mod[l], ln2_g[l], ln2_b[l], mod[min(l + 1, depth - 1)], nc)
    return xc[:nl][None]
```

```python
import functools
import math

import numpy as np
import jax
import jax.numpy as jnp
from jax import lax
from jax.experimental import pallas as pl
from jax.experimental.pallas import tpu as pltpu

F32 = jnp.float32
BF16 = jnp.bfloat16
I32 = jnp.int32
HI = lax.Precision.HIGHEST

D_MODEL = 2048
GRID_W = 64
HEAD_DIM = 128
LANES = 128
BW = D_MODEL // 2
A_HEADS, A_KV = 8, 2
A_WINDOW = 128
A_BLK = 128
B_HEADS = 8
B_CHUNK = 64
D_HEADS = 8
NA_KH, NA_KW = 8, 16
N_EXPERTS = 16
EXPERT_FF = D_MODEL // 2
EC_CAPACITY = 2
HYENA_EMB, HYENA_FFN = 33, 64
ROPE_BASE = 10000.0
LN_EPS = 1e-6
NORM_EPS = 1e-6
NEG_INF = -1e30
DEPTH = 4
ALPHA = (2.0 * DEPTH) ** 0.25

SEG1 = 5632
SEG_AB = 32
SEG3 = 14336
FFT_L = 128

VMEM_LIMIT = 56 * 1024 * 1024


def _cp(sem=None, vmem=VMEM_LIMIT):
    kw = dict(vmem_limit_bytes=vmem)
    if sem is not None:
        kw["dimension_semantics"] = sem
    return pltpu.CompilerParams(**kw)


def _dot(a, b):
    return jnp.dot(a, b, preferred_element_type=F32)


def _dot_hi(a, b):
    return jnp.dot(a, b, preferred_element_type=F32, precision=HI)


def _dot_nt(a, b):
    return lax.dot_general(a, b, (((1,), (1,)), ((), ())), preferred_element_type=F32)


def _dot_nt_hi(a, b):
    return lax.dot_general(a, b, (((1,), (1,)), ((), ())), preferred_element_type=F32, precision=HI)


def _dot_tn(a, b):
    return lax.dot_general(a, b, (((0,), (0,)), ((), ())), preferred_element_type=F32)


def _silu(x):
    return x * jax.nn.sigmoid(x)


def _modvec_kernel(ct_ref, w_ref, b_ref, o_ref):
    s = _silu(ct_ref[...])
    w = w_ref[...]
    r0 = jnp.sum(w * s[:, 0:1], axis=0, keepdims=True) + b_ref[...]
    r1 = jnp.sum(w * s[:, 1:2], axis=0, keepdims=True) + b_ref[...]
    o_ref[...] = jnp.concatenate([r0, r1, jnp.zeros((6, w.shape[1]), F32)], axis=0)


def _modvec(c, c_ctx, w_mod, b_mod):
    depth, d, n6 = w_mod.shape
    tn = 512
    ct = jnp.zeros((d, LANES), F32).at[:, 0].set(c[0]).at[:, 1].set(c_ctx)
    return pl.pallas_call(
        _modvec_kernel,
        grid=(depth, n6 // tn),
        in_specs=[pl.BlockSpec((d, LANES), lambda l, j: (0, 0)),
                  pl.BlockSpec((None, d, tn), lambda l, j: (l, 0, j)),
                  pl.BlockSpec((None, 1, tn), lambda l, j: (l, 0, j))],
        out_specs=pl.BlockSpec((None, 8, tn), lambda l, j: (l, 0, j)),
        out_shape=jax.ShapeDtypeStruct((depth, 8, n6), F32),
        compiler_params=_cp(("parallel", "parallel")),
        name="modvec",
    )(ct, w_mod, b_mod.reshape(depth, 1, n6))


def _pick_mod(m, is_ctx):
    return jnp.where(is_ctx, m[1:2, :], m[0:1, :])


def _modulate_kernel(x_ref, sh_ref, sc_ref, o_ref):
    is_ctx = pl.program_id(0) == pl.num_programs(0) - 1
    sh = _pick_mod(sh_ref[...], is_ctx)
    sc = _pick_mod(sc_ref[...], is_ctx)
    o_ref[...] = (x_ref[...] * (1.0 + sc) + sh).astype(o_ref.dtype)


def _modulate(x, mod_l, nc):
    nt, d = x.shape
    return pl.pallas_call(
        _modulate_kernel,
        grid=(nt // nc,),
        in_specs=[pl.BlockSpec((nc, d), lambda i: (i, 0)),
                  pl.BlockSpec((8, d), lambda i: (0, 0)),
                  pl.BlockSpec((8, d), lambda i: (0, 1))],
        out_specs=pl.BlockSpec((nc, d), lambda i: (i, 0)),
        out_shape=jax.ShapeDtypeStruct((nt, d), BF16),
        compiler_params=_cp(("parallel",)),
        name="modulate",
    )(x, mod_l, mod_l)


def _mm_kernel(a_ref, w_ref, o_ref):
    o_ref[...] = _dot(a_ref[...], w_ref[...].astype(BF16)).astype(o_ref.dtype)


def _matmul(a, w, l, col0, n, tm, tn, out_dtype=F32, name="mm"):
    m, k = a.shape
    assert m % tm == 0 and n % tn == 0 and col0 % tn == 0
    cb = col0 // tn
    return pl.pallas_call(
        _mm_kernel,
        grid=(m // tm, n // tn),
        in_specs=[pl.BlockSpec((tm, k), lambda i, j: (i, 0)),
                  pl.BlockSpec((None, k, tn), lambda i, j: (l, 0, cb + j))],
        out_specs=pl.BlockSpec((tm, tn), lambda i, j: (i, j)),
        out_shape=jax.ShapeDtypeStruct((m, n), out_dtype),
        compiler_params=_cp(("parallel", "parallel")),
        name=name,
    )(a, w)


def _rope_tables(nl, nc):
    half = HEAD_DIM // 2
    nf = half // 2
    inv_freq = ROPE_BASE ** (-jnp.arange(nf, dtype=F32) / nf)
    tpos = jnp.arange(nl)
    ang_r = (tpos // GRID_W).astype(F32)[:, None] * inv_freq[None, :]
    ang_c = (tpos % GRID_W).astype(F32)[:, None] * inv_freq[None, :]
    cr, sr, cc, sc = jnp.cos(ang_r), jnp.sin(ang_r), jnp.cos(ang_c), jnp.sin(ang_c)
    cos_t = jnp.concatenate([cr, cr, cc, cc], axis=1)
    sin_t = jnp.concatenate([-sr, sr, -sc, sc], axis=1)
    cos_t = jnp.concatenate([cos_t, jnp.ones((nc, HEAD_DIM), F32)], axis=0)
    sin_t = jnp.concatenate([sin_t, jnp.zeros((nc, HEAD_DIM), F32)], axis=0)
    return cos_t, sin_t


def _rope_kernel(x_ref, c_ref, s_ref, o_ref, *, n_rot, n_sl):
    cos_t = c_ref[...]
    sin_t = s_ref[...]
    lane = lax.broadcasted_iota(I32, cos_t.shape, 1)
    quarter = HEAD_DIM // 4
    first = (lane % (2 * quarter)) < quarter
    for j in range(n_sl):
        sl = slice(j * HEAD_DIM, (j + 1) * HEAD_DIM)
        x = x_ref[:, sl]
        if j < n_rot:
            swapped = jnp.where(first, pltpu.roll(x, HEAD_DIM - quarter, 1), pltpu.roll(x, quarter, 1))
            x = x * cos_t + swapped * sin_t
        o_ref[:, sl] = x.astype(o_ref.dtype)


def _rope(pa, cos_t, sin_t, tr):
    nt = pa.shape[0]
    n_rot = A_HEADS + A_KV
    n_sl = A_HEADS + 2 * A_KV
    width = n_sl * HEAD_DIM
    return pl.pallas_call(
        functools.partial(_rope_kernel, n_rot=n_rot, n_sl=n_sl),
        grid=(nt // tr,),
        in_specs=[pl.BlockSpec((tr, width), lambda i: (i, 0)),
                  pl.BlockSpec((tr, HEAD_DIM), lambda i: (i, 0)),
                  pl.BlockSpec((tr, HEAD_DIM), lambda i: (i, 0))],
        out_specs=pl.BlockSpec((tr, width), lambda i: (i, 0)),
        out_shape=jax.ShapeDtypeStruct((nt, width), BF16),
        compiler_params=_cp(("parallel",)),
        name="rope",
    )(pa, cos_t, sin_t)


def _attn_a_kernel(sink_ref, q_ref, k_ref, v_ref, o_ref, *, nl, nc):
    h = pl.program_id(0)
    qb = pl.program_id(1)
    grp = A_HEADS // A_KV
    scale = HEAD_DIM ** -0.5
    q = q_ref[...]
    qs = jnp.concatenate([q[:, g * HEAD_DIM:(g + 1) * HEAD_DIM] for g in range(grp)], axis=0)
    rows = grp * A_BLK
    wlen = 3 * A_BLK
    is_lat = qb < nl // A_BLK
    ws = pl.multiple_of(jnp.clip((qb - 1) * A_BLK, 0, nl - wlen), A_BLK)
    kc = k_ref[nl:nl + nc, :]
    vc = v_ref[nl:nl + nc, :]
    kw = k_ref[pl.ds(ws, wlen), :]
    vw = v_ref[pl.ds(ws, wlen), :]
    s_c = _dot_nt(qs, kc) * scale
    s_l = _dot_nt(qs, kw) * scale
    ri = lax.broadcasted_iota(I32, (rows, wlen), 0)
    ci = lax.broadcasted_iota(I32, (rows, wlen), 1)
    qabs = qb * A_BLK + (ri % A_BLK)
    kabs = ws + ci
    mask = is_lat & (jnp.abs(kabs - qabs) <= A_WINDOW)
    s_l = jnp.where(mask, s_l, NEG_INF)
    rcol = lax.broadcasted_iota(I32, (rows, 1), 0) // A_BLK
    sink = jnp.zeros((rows, 1), F32)
    for g in range(grp):
        sink = jnp.where(rcol == g, sink_ref[h * grp + g], sink)
    m = jnp.maximum(jnp.maximum(jnp.max(s_c, axis=-1, keepdims=True),
                                jnp.max(s_l, axis=-1, keepdims=True)), sink)
    p_c = jnp.exp(s_c - m)
    p_l = jnp.exp(s_l - m)
    den = (jnp.sum(p_c, axis=-1, keepdims=True) + jnp.sum(p_l, axis=-1, keepdims=True)
           + jnp.exp(sink - m))
    y = (_dot(p_c.astype(BF16), vc) + _dot(p_l.astype(BF16), vw)) / den
    for g in range(grp):
        o_ref[:, g * HEAD_DIM:(g + 1) * HEAD_DIM] = y[g * A_BLK:(g + 1) * A_BLK].astype(o_ref.dtype)


def _attn_a(qkv, sink, nl, nc):
    nt = qkv.shape[0]
    grp = A_HEADS // A_KV
    qw = grp * HEAD_DIM
    kcol = A_HEADS
    vcol = A_HEADS + A_KV
    return pl.pallas_call(
        functools.partial(_attn_a_kernel, nl=nl, nc=nc),
        grid=(A_KV, nt // A_BLK),
        in_specs=[pl.BlockSpec(memory_space=pltpu.SMEM),
                  pl.BlockSpec((A_BLK, qw), lambda h, b: (b, h)),
                  pl.BlockSpec((nt, HEAD_DIM), lambda h, b: (0, kcol + h)),
                  pl.BlockSpec((nt, HEAD_DIM), lambda h, b: (0, vcol + h))],
        out_specs=pl.BlockSpec((A_BLK, qw), lambda h, b: (b, h)),
        out_shape=jax.ShapeDtypeStruct((nt, A_HEADS * HEAD_DIM), BF16),
        compiler_params=_cp(("parallel", "arbitrary")),
        name="attn_a",
    )(sink, qkv, qkv, qkv)


def _na_bias_table(rel_bias):
    cq = jnp.arange(GRID_W)
    col_start = jnp.clip(cq - NA_KW // 2, 0, GRID_W - NA_KW)
    col_ok = (cq[None, :] >= col_start[:, None]) & (cq[None, :] < col_start[:, None] + NA_KW)
    dc = jnp.clip(cq[None, :] - cq[:, None], -(NA_KW - 1), NA_KW - 1) + (NA_KW - 1)
    b = rel_bias.astype(F32)[:, :, dc]
    b = jnp.where(col_ok[None, None], b, NEG_INF)
    rows = [jnp.concatenate([b[:, d0 + j] for j in range(NA_KH)], axis=-1) for d0 in range(NA_KH)]
    return jnp.stack(rows, axis=1)


def _na_kernel(q_ref, k_ref, v_ref, bw_ref, o_ref, *, nl, nc):
    qb = pl.program_id(1)
    scale = HEAD_DIM ** -0.5
    rows = nl // GRID_W
    rpb = nc // GRID_W
    kc = k_ref[nl:nl + nc, :].astype(BF16)
    vc = v_ref[nl:nl + nc, :].astype(BF16)

    @pl.when(qb == nl // nc)
    def _():
        q = q_ref[...].astype(BF16)
        s = _dot_nt(q, kc) * scale
        m = jnp.max(s, axis=-1, keepdims=True)
        p = jnp.exp(s - m)
        den = jnp.sum(p, axis=-1, keepdims=True)
        o_ref[...] = (_dot(p.astype(BF16), vc) / den).astype(o_ref.dtype)

    @pl.when(qb < nl // nc)
    def _():
        q_all = q_ref[...].astype(BF16)
        s_c_all = _dot_nt(q_all, kc) * scale
        rows_ = []
        for i in range(rpb):
            r = qb * rpb + i
            st = jnp.clip(r - NA_KH // 2, 0, rows - NA_KH)
            w0 = pl.multiple_of(st * GRID_W, GRID_W)
            rows_.append(dict(
                q=q_all[i * GRID_W:(i + 1) * GRID_W], s_c=s_c_all[i * GRID_W:(i + 1) * GRID_W],
                kw=k_ref[pl.ds(w0, NA_KH * GRID_W), :].astype(BF16),
                vw=v_ref[pl.ds(w0, NA_KH * GRID_W), :].astype(BF16),
                bias=bw_ref[st - r + (NA_KH - 1)]))
        for e in rows_:
            e["s_l"] = _dot_nt(e["q"], e["kw"]) * scale + e["bias"]
        for e in rows_:
            m = jnp.maximum(jnp.max(e["s_c"], axis=-1, keepdims=True), jnp.max(e["s_l"], axis=-1, keepdims=True))
            p_c = jnp.exp(e["s_c"] - m)
            p_l = jnp.exp(e["s_l"] - m)
            e["den"] = jnp.sum(p_c, axis=-1, keepdims=True) + jnp.sum(p_l, axis=-1, keepdims=True)
            e["p_c"] = p_c.astype(BF16)
            e["p_l"] = p_l.astype(BF16)
        y_c = _dot(jnp.concatenate([e["p_c"] for e in rows_], axis=0), vc)
        y_l = [_dot(e["p_l"], e["vw"]) for e in rows_]
        for i, e in enumerate(rows_):
            y = (y_c[i * GRID_W:(i + 1) * GRID_W] + y_l[i]) / e["den"]
            o_ref[i * GRID_W:(i + 1) * GRID_W, :] = y.astype(o_ref.dtype)


def _na(pc, bw, nl, nc, qcol):
    nt = pc.shape[0]
    qc = qcol // HEAD_DIM
    kcb = qc + D_HEADS
    vcb = qc + 2 * D_HEADS
    return pl.pallas_call(
        functools.partial(_na_kernel, nl=nl, nc=nc),
        grid=(D_HEADS, nt // nc),
        in_specs=[pl.BlockSpec((nc, HEAD_DIM), lambda h, b: (b, qc + h)),
                  pl.BlockSpec((nt, HEAD_DIM), lambda h, b: (0, kcb + h)),
                  pl.BlockSpec((nt, HEAD_DIM), lambda h, b: (0, vcb + h)),
                  pl.BlockSpec((None, NA_KH, GRID_W, NA_KH * GRID_W), lambda h, b: (h, 0, 0, 0))],
        out_specs=pl.BlockSpec((nc, HEAD_DIM), lambda h, b: (b, h)),
        out_shape=jax.ShapeDtypeStruct((nt, D_HEADS * HEAD_DIM), BF16),
        compiler_params=_cp(("parallel", "arbitrary")),
        name="na_attn",
    )(pc, pc, pc, bw)


def _conv3(x_ref, p_ref, n_ref, w_ref, scr, nb_lat):
    i = pl.program_id(0)
    r = x_ref.shape[0]
    x = x_ref[...]
    has_prev = (i >= 1) & (i < nb_lat)
    has_next = i < nb_lat - 1
    scr[8:8 + r, :] = x
    scr[7:8, :] = jnp.where(has_prev, p_ref[7:8, :], 0.0)
    scr[8 + r:9 + r, :] = jnp.where(has_next, n_ref[0:1, :], 0.0)
    w = w_ref[...]
    return w[0:1, :] * scr[7:7 + r, :] + w[1:2, :] * x + w[2:3, :] * scr[9:9 + r, :]


def _halo_specs(nc, tc, cb, nt):
    r8 = nc // 8
    last8 = nt // 8 - 1
    return [pl.BlockSpec((nc, tc), lambda i, j: (i, cb + j)),
            pl.BlockSpec((8, tc), lambda i, j: (jnp.maximum(i * r8 - 1, 0), cb + j)),
            pl.BlockSpec((8, tc), lambda i, j: (jnp.minimum((i + 1) * r8, last8), cb + j))]


def _dprep_kernel(x_ref, p_ref, n_ref, w_ref, o_ref, scr, *, nb_lat, n_norm):
    y = _silu(_conv3(x_ref, p_ref, n_ref, w_ref, scr, nb_lat))
    j = pl.program_id(1)
    tc = y.shape[1]
    for s in range(tc // HEAD_DIM):
        ys = y[:, s * HEAD_DIM:(s + 1) * HEAD_DIM]
        nrm = ys * lax.rsqrt(jnp.sum(ys * ys, axis=-1, keepdims=True) + NORM_EPS)
        o_ref[:, s * HEAD_DIM:(s + 1) * HEAD_DIM] = jnp.where(j < n_norm, nrm, ys)


def _dprep(pa, conv_w, nl, nc, col0):
    nt = pa.shape[0]
    tc = 512
    width = 3 * BW
    return pl.pallas_call(
        functools.partial(_dprep_kernel, nb_lat=nl // nc, n_norm=2 * BW // tc),
        grid=(nt // nc, width // tc),
        in_specs=_halo_specs(nc, tc, col0 // tc, nt) + [pl.BlockSpec((3, tc), lambda i, j: (0, j))],
        out_specs=pl.BlockSpec((nc, tc), lambda i, j: (i, j)),
        out_shape=jax.ShapeDtypeStruct((nt, width), F32),
        scratch_shapes=[pltpu.VMEM((nc + 16, tc), F32)],
        compiler_params=_cp(("parallel", "arbitrary")),
        name="delta_prep",
    )(pa, pa, pa, conv_w)


def _softplus(x):
    return jnp.maximum(x, 0.0) + jnp.log1p(jnp.exp(-jnp.abs(x)))


def _mxu(a, b, passes=1):
    ah = a.astype(BF16)
    bh = b.astype(BF16)
    out = _dot(ah, bh)
    if passes == 3:
        al = (a - ah.astype(F32)).astype(BF16)
        bl = (b - bh.astype(F32)).astype(BF16)
        out = out + _dot(al, bh) + _dot(ah, bl)
    return out


DELTA_SOLVE_PASSES = 1


def _delta_kernel(qf_ref, abf_ref, qb_ref, abb_ref, alog_ref, dtb_ref, of_ref, ob_ref, s_ref):
    ck = B_CHUNK
    nh = B_HEADS
    p = DELTA_SOLVE_PASSES

    @pl.when(pl.program_id(0) == 0)
    def _():
        s_ref[...] = jnp.zeros_like(s_ref)

    ri = lax.broadcasted_iota(I32, (ck, ck), 0)
    ci = lax.broadcasted_iota(I32, (ck, ck), 1)
    chains = []
    for d, (qkv_ref, ab_ref, o_ref) in enumerate(((qf_ref, abf_ref, of_ref), (qb_ref, abb_ref, ob_ref))):
        incl = (ri >= ci) if d == 0 else (ri <= ci)
        strict = (ri > ci) if d == 0 else (ri < ci)
        ab = ab_ref[...]
        dsl = slice(d * nh, (d + 1) * nh)
        g8 = -jnp.exp(alog_ref[:, dsl]) * _softplus(ab[:, dsl] + dtb_ref[:, dsl])
        b8 = jax.nn.sigmoid(ab[:, 2 * nh + d * nh:2 * nh + (d + 1) * nh])
        cum8 = _dot_hi(incl.astype(F32), g8)
        tot8 = jnp.sum(g8, axis=0, keepdims=True)
        for h in range(nh):
            gc = cum8[:, h:h + 1]
            gm = jnp.broadcast_to(gc, (ck, ck))
            chains.append(dict(
                qkv=qkv_ref, o=o_ref, h=h, si=d * nh + h, incl=incl, strict=strict, gc=gc, bc=b8[:, h:h + 1],
                tot=tot8[:, h:h + 1], decay=jnp.exp(jnp.where(incl, gm - gm.T, NEG_INF)), eg=jnp.exp(gc)))

    def head(c, part):
        off = part * BW + c["h"] * HEAD_DIM
        return c["qkv"][:, off:off + HEAD_DIM]

    for c in chains:
        k = head(c, 1)
        c["kb"] = k * c["bc"]
        c["kf"] = k.astype(BF16)
    for c in chains:
        c["a"] = jnp.where(c["strict"], _dot_nt(c["kb"].astype(BF16), c["kf"]) * c["decay"], 0.0)
    for c in chains:
        c["rhs"] = jnp.concatenate([head(c, 2) * c["bc"], c["kb"] * c["eg"]], axis=1)
    for c in chains:
        c["sol"] = c["rhs"] - _mxu(c["a"], c["rhs"], p)
        c["pw"] = c["a"]
    for _ in range(5):
        for c in chains:
            c["pw"] = _mxu(c["pw"], c["pw"], p)
        for c in chains:
            c["sol"] = c["sol"] + _mxu(c["pw"], c["sol"], p)
    for c in chains:
        c["q"] = head(c, 0) * (HEAD_DIM ** -0.5)
        c["qk"] = jnp.where(c["incl"], _dot_nt(c["q"].astype(BF16), c["kf"]) * c["decay"], 0.0)
    for c in chains:
        c["st"] = s_ref[c["si"]]
        c["stb"] = c["st"].astype(BF16)
        c["vn"] = (c["sol"][:, :HEAD_DIM] - _dot(c["sol"][:, HEAD_DIM:].astype(BF16), c["stb"])).astype(BF16)
    for c in chains:
        sl = slice(c["h"] * HEAD_DIM, (c["h"] + 1) * HEAD_DIM)
        c["o"][:, sl] = (_dot((c["q"] * c["eg"]).astype(BF16), c["stb"])
                         + _dot(c["qk"].astype(BF16), c["vn"]))
    for c in chains:
        kd = head(c, 1) * jnp.exp(c["tot"] - c["gc"])
        s_ref[c["si"]] = c["st"] * jnp.exp(c["tot"]) + _dot_tn(kd.astype(BF16), c["vn"])


def _delta(qkvb, pab, a_log, dt_bias, nl, nc):
    nt = qkvb.shape[0]
    ck = B_CHUNK
    nlc, ncc = nl // ck, nc // ck
    ntc = nlc + ncc
    rf = lambda c: jnp.where(c < ncc, nlc + c, c - ncc)
    rb = lambda c: jnp.where(c < ncc, nlc + ncc - 1 - c, nlc - 1 - (c - ncc))
    vec = pl.BlockSpec((1, 2 * B_HEADS), lambda c: (0, 0))
    out = jax.ShapeDtypeStruct((nt, BW), F32)
    return pl.pallas_call(
        _delta_kernel,
        grid=(ntc,),
        in_specs=[pl.BlockSpec((ck, 3 * BW), lambda c: (rf(c), 0)),
                  pl.BlockSpec((ck, 4 * B_HEADS), lambda c: (rf(c), 0)),
                  pl.BlockSpec((ck, 3 * BW), lambda c: (rb(c), 0)),
                  pl.BlockSpec((ck, 4 * B_HEADS), lambda c: (rb(c), 0)),
                  vec, vec],
        out_specs=[pl.BlockSpec((ck, BW), lambda c: (rf(c), 0)), pl.BlockSpec((ck, BW), lambda c: (rb(c), 0))],
        out_shape=[out, out],
        scratch_shapes=[pltpu.VMEM((2 * B_HEADS, HEAD_DIM, HEAD_DIM), F32)],
        compiler_params=_cp(("arbitrary",)),
        name="delta_scan",
    )(qkvb, pab, qkvb, pab, a_log.reshape(1, 2 * B_HEADS), dt_bias.reshape(1, 2 * B_HEADS))


def _dpost_kernel(of_ref, ob_ref, g_ref, ng_ref, o_ref):
    o = of_ref[...] + ob_ref[...]
    gate = g_ref[...]
    ng = ng_ref[...]
    for s in range(o.shape[1] // HEAD_DIM):
        sl = slice(s * HEAD_DIM, (s + 1) * HEAD_DIM)
        os_ = o[:, sl]
        y = os_ * lax.rsqrt(jnp.mean(os_ * os_, axis=-1, keepdims=True) + NORM_EPS) * ng
        o_ref[:, sl] = (y * _silu(gate[:, sl])).astype(o_ref.dtype)


def _dpost(o_f, o_b, pa, norm_g, nc, gcol):
    nt = pa.shape[0]
    tc = 512
    gb = gcol // tc
    return pl.pallas_call(
        _dpost_kernel,
        grid=(nt // nc, BW // tc),
        in_specs=[pl.BlockSpec((nc, tc), lambda i, j: (i, j)),
                  pl.BlockSpec((nc, tc), lambda i, j: (i, j)),
                  pl.BlockSpec((nc, tc), lambda i, j: (i, gb + j)),
                  pl.BlockSpec((1, HEAD_DIM), lambda i, j: (0, 0))],
        out_specs=pl.BlockSpec((nc, tc), lambda i, j: (i, j)),
        out_shape=jax.ShapeDtypeStruct((nt, BW), BF16),
        compiler_params=_cp(("parallel", "parallel")),
        name="delta_post",
    )(o_f, o_b, pa, norm_g.reshape(1, HEAD_DIM))


def _hconv_kernel(a_ref, ap_ref, an_ref, b_ref, bp_ref, bn_ref, c_ref, cp_ref, cn_ref,
                  wa_ref, wb_ref, wc_ref, x0_ref, z_ref, scr, *, nb_lat):
    x0_ref[...] = _conv3(a_ref, ap_ref, an_ref, wa_ref, scr, nb_lat)
    x1 = _conv3(b_ref, bp_ref, bn_ref, wb_ref, scr, nb_lat)
    v = _conv3(c_ref, cp_ref, cn_ref, wc_ref, scr, nb_lat)
    z_ref[...] = v * x1


def _hconv(pc, conv_w, nl, nc):
    nt = pc.shape[0]
    tc = 512
    gpb = BW // tc
    specs = []
    for g in range(3):
        specs += _halo_specs(nc, tc, g * gpb, nt)
    for g in range(3):
        specs.append(pl.BlockSpec((3, tc), lambda i, j, g=g: (0, g * gpb + j)))
    out = pl.BlockSpec((nc, tc), lambda i, j: (i, j))
    return pl.pallas_call(
        functools.partial(_hconv_kernel, nb_lat=nl // nc),
        grid=(nt // nc, gpb),
        in_specs=specs,
        out_specs=[out, out],
        out_shape=[jax.ShapeDtypeStruct((nt, BW), F32)] * 2,
        scratch_shapes=[pltpu.VMEM((nc + 16, tc), F32)],
        compiler_params=_cp(("parallel", "arbitrary")),
        name="hyena_conv",
    )(pc, pc, pc, pc, pc, pc, pc, pc, pc, conv_w, conv_w, conv_w)


def _hfilter_features(n):
    t = np.linspace(0.0, 1.0, n, dtype=np.float32)[:, None]
    bands = (HYENA_EMB - 1) // 2
    wpos = (2.0 * math.pi * np.arange(n, dtype=np.float32)[:, None] / n).astype(np.float32)
    f = np.linspace(1e-4, bands - 1, bands, dtype=np.float32)[None, :]
    z = np.concatenate([t, np.cos(f * wpos), -np.sin(f * wpos)], axis=-1).astype(np.float32)
    pos = np.concatenate([np.arange(n), [0], 2 * n - np.arange(n + 1, 2 * n)])
    zk = np.zeros((2 * n, LANES), np.float32)
    zk[:, :HYENA_EMB] = z[pos]
    max_decay = math.log(1e-2) / 0.3
    min_decay = math.log(1e-2) / 1.5
    absd = np.abs(np.linspace(min_decay, max_decay, BW, dtype=np.float32))[None, :]
    return jnp.asarray(zk), jnp.asarray(absd)


def _hfilter_kernel(z_ref, w1, b1, fr, w2, b2, w3, b3, w4, b4, ad, o_ref, *, n, tr):
    z = z_ref[...]
    frq = fr[...]
    h = jnp.sin(frq * (_mxu(z, w1[...], 3) + b1[...]))
    h = jnp.sin(frq * (_mxu(h, w2[...], 3) + b2[...]))
    h = jnp.sin(frq * (_mxu(h, w3[...], 3) + b3[...]))
    h = _mxu(h, w4[...], 3) + b4[...]
    win = jnp.exp(-z[:, 0:1] * ad[...])
    row = pl.program_id(0) * tr + lax.broadcasted_iota(I32, (tr, 1), 0)
    o_ref[...] = jnp.where(row == n, 0.0, h * win)


def _hfilter(n, zk, absd, w1, b1, fr, w2, b2, w3, b3, w4, b4):
    tr = min(512, n)
    nfb = n // tr
    w1p = jnp.zeros((LANES, HYENA_FFN), F32).at[:HYENA_EMB].set(w1)
    small = lambda shp: pl.BlockSpec(shp, lambda i: (0, 0))
    row = lambda a: a.reshape(1, -1)
    return pl.pallas_call(
        functools.partial(_hfilter_kernel, n=n, tr=tr),
        grid=(2 * n // tr,),
        in_specs=[pl.BlockSpec((tr, LANES), lambda i: (i, 0)),
                  small((LANES, HYENA_FFN)), small((1, HYENA_FFN)), small((1, HYENA_FFN)),
                  small((HYENA_FFN, HYENA_FFN)), small((1, HYENA_FFN)),
                  small((HYENA_FFN, HYENA_FFN)), small((1, HYENA_FFN)),
                  pl.BlockSpec((HYENA_FFN, BW), lambda i: (0, (i >= nfb).astype(I32))),
                  pl.BlockSpec((1, BW), lambda i: (0, (i >= nfb).astype(I32))),
                  small((1, BW))],
        out_specs=pl.BlockSpec((tr, BW), lambda i: (i, 0)),
        out_shape=jax.ShapeDtypeStruct((2 * n, BW), F32),
        compiler_params=_cp(("parallel",)),
        name="hyena_filter",
    )(zk, w1p, row(b1), row(fr), w2, row(b2), w3, row(b3), w4, row(b4), absd)


def _fft_tables(n):
    big = 2 * n
    h = big // FFT_L
    l2 = FFT_L // 2

    def cs(a, b, period):
        ang = 2.0 * np.pi * np.outer(np.arange(a), np.arange(b)) / period
        return np.cos(ang), np.sin(ang)

    ch, sh = cs(h, h, h)
    cl, sl = cs(FFT_L, FFT_L, FFT_L)
    ct, st = cs(h, FFT_L, big)
    f = lambda a: jnp.asarray(a, F32)
    return dict(
        fhr=f(ch), fhi=f(-sh),
        flr=f(cl[:l2]), fli=f(-sl[:l2]),
        twr=f(ct), twi=f(-st),
        clr=f(cl[:, :l2]), cli=f(sl[:, :l2]),
        chr=f(ch[:h // 2]), chi=f(sh[:h // 2]),
        ctr=f(ct.T), cti=f(st.T),
    )


def _alt_sum(x):
    row = lax.broadcasted_iota(I32, x.shape, 0)
    return jnp.sum(jnp.where((row & 1) == 1, -x, x), axis=0, keepdims=True)


FFT_PITCH = FFT_L + 8
FFT_GROUP = 8


def _lane_halves(x):
    return x[:, :LANES], x[:, LANES:]


def _fft_stage1(x_ref, vr_ref, vi_ref, fhr_ref, fhi_ref, h_in, h):
    fs = jnp.concatenate([fhr_ref[:, 0:h_in], fhi_ref[:, 0:h_in]], axis=0).astype(BF16)
    g = FFT_GROUP

    def body(step, carry):
        m0 = step * g
        cols = [x_ref[pl.ds(m0 + i, h_in, stride=FFT_L), :].astype(BF16) for i in range(g)]
        outs = [_dot(fs, jnp.concatenate(cols[i:i + 2], axis=1)) for i in range(0, g, 2)]
        for i in range(g):
            o = _lane_halves(outs[i // 2])[i % 2]
            vr_ref[pl.ds(m0 + i, h, stride=FFT_PITCH), :] = o[:h]
            vi_ref[pl.ds(m0 + i, h, stride=FFT_PITCH), :] = o[h:]
        return carry

    lax.fori_loop(0, FFT_L // g, body, 0)


def _fft_stage2_blocks(f0, n_blk, vr_ref, vi_ref, flr_ref, fli_ref, twr_ref, twi_ref):
    flr = flr_ref[...]
    fli = fli_ref[...]
    mats, ops = [], []
    for i in range(n_blk):
        twr = twr_ref[pl.ds(f0 + i, 1), :]
        twi = twi_ref[pl.ds(f0 + i, 1), :]
        mats.append(((flr * twr - fli * twi).astype(BF16), (flr * twi + fli * twr).astype(BF16)))
        r0 = pl.multiple_of((f0 + i) * FFT_PITCH, 8)
        ops.append(jnp.concatenate([vr_ref[pl.ds(r0, FFT_L), :], vi_ref[pl.ds(r0, FFT_L), :]],
                                   axis=1).astype(BF16))
    pa = [_dot(m[0], v) for m, v in zip(mats, ops)]
    pb = [_dot(m[1], v) for m, v in zip(mats, ops)]
    out = []
    for a, b in zip(pa, pb):
        ar, ai = _lane_halves(a)
        br, bi = _lane_halves(b)
        out.append((ar - bi, ai + br))
    return out


def _kfft_kernel(k_ref, fhr, fhi, flr, fli, twr, twi, kr_ref, ki_ref, kn_ref, vr_ref, vi_ref, *, h):
    l2 = FFT_L // 2
    g = min(FFT_GROUP, h)
    kn_ref[...] = _alt_sum(k_ref[...])
    _fft_stage1(k_ref, vr_ref, vi_ref, fhr, fhi, h, h)

    def body(step, carry):
        f0 = step * g
        for i, (xr, xi) in enumerate(_fft_stage2_blocks(f0, g, vr_ref, vi_ref, flr, fli, twr, twi)):
            k0 = pl.multiple_of((f0 + i) * l2, l2)
            kr_ref[pl.ds(k0, l2), :] = xr
            ki_ref[pl.ds(k0, l2), :] = xi
        return carry

    lax.fori_loop(0, h // g, body, 0)


def _table_specs(tabs, names):
    return [pl.BlockSpec(tabs[k].shape, lambda j: (0, 0)) for k in names]


def _kfft(kt, tabs, n):
    big = 2 * n
    h = big // FFT_L
    l2 = FFT_L // 2
    names = ["fhr", "fhi", "flr", "fli", "twr", "twi"]
    spec = pl.BlockSpec((h * l2, LANES), lambda j: (0, j))
    return pl.pallas_call(
        functools.partial(_kfft_kernel, h=h),
        grid=(BW // LANES,),
        in_specs=[pl.BlockSpec((big, LANES), lambda j: (0, j))] + _table_specs(tabs, names),
        out_specs=[spec, spec, pl.BlockSpec((1, LANES), lambda j: (0, j))],
        out_shape=[jax.ShapeDtypeStruct((h * l2, BW), F32)] * 2 + [jax.ShapeDtypeStruct((1, BW), F32)],
        scratch_shapes=[pltpu.VMEM((h * FFT_PITCH, LANES), F32)] * 2,
        compiler_params=_cp(("parallel",)),
        name="hyena_filter_fft",
    )(kt, *[tabs[k] for k in names])


def _zfft_kernel(z_ref, x0_ref, kr_ref, ki_ref, kn_ref, d_ref, fhr, fhi, flr, fli, twr, twi,
                 clr, cli, chr_, chi, ctr, cti, o_ref, vr_ref, vi_ref, *, h):
    l2 = FFT_L // 2
    hh = h // 2
    big = h * FFT_L
    pn = _alt_sum(z_ref[...]) * kn_ref[...] * (1.0 / big)
    _fft_stage1(z_ref, vr_ref, vi_ref, fhr, fhi, hh, h)
    cr = clr[...].astype(BF16)
    ci = cli[...].astype(BF16)
    rowi = lax.broadcasted_iota(I32, (l2, 1), 0)
    gm = min(FFT_GROUP // 2, h)

    def mid(step, carry):
        f0 = step * gm
        prods = []
        for i, (xr, xi) in enumerate(_fft_stage2_blocks(f0, gm, vr_ref, vi_ref, flr, fli, twr, twi)):
            k0 = pl.multiple_of((f0 + i) * l2, l2)
            kr = kr_ref[pl.ds(k0, l2), :]
            ki = ki_ref[pl.ds(k0, l2), :]
            half_dc = jnp.where((rowi == 0) & (f0 + i == 0), 0.5, 1.0)
            prods.append(jnp.concatenate([(xr * kr - xi * ki) * half_dc, (xr * ki + xi * kr) * half_dc],
                                         axis=1).astype(BF16))
        za = [_dot(cr, p) for p in prods]
        zb = [_dot(ci, p) for p in prods]
        for i in range(gm):
            ar, ai = _lane_halves(za[i])
            br, bi = _lane_halves(zb[i])
            r0 = pl.multiple_of((f0 + i) * FFT_PITCH, 8)
            vr_ref[pl.ds(r0, FFT_L), :] = ar - bi
            vi_ref[pl.ds(r0, FFT_L), :] = ai + br
        return carry

    lax.fori_loop(0, h // gm, mid, 0)
    ch_r = chr_[...]
    ch_i = chi[...]
    dvec = d_ref[...]
    gl = FFT_GROUP

    def last(step, carry):
        m0 = step * gl
        mats, ops = [], []
        for i in range(gl):
            tr_ = ctr[pl.ds(m0 + i, 1), :]
            ti_ = cti[pl.ds(m0 + i, 1), :]
            mats.append(jnp.concatenate([ch_r * tr_ - ch_i * ti_, -(ch_r * ti_ + ch_i * tr_)], axis=1).astype(BF16))
            ops.append(jnp.concatenate([vr_ref[pl.ds(m0 + i, h, stride=FFT_PITCH), :],
                                        vi_ref[pl.ds(m0 + i, h, stride=FFT_PITCH), :]], axis=0).astype(BF16))
        ys = [_dot(m, v) for m, v in zip(mats, ops)]
        for i in range(gl):
            sgn = jnp.where(((m0 + i) & 1) == 1, -1.0, 1.0)
            y = ys[i] * (2.0 / big) + pn * sgn
            zt = z_ref[pl.ds(m0 + i, hh, stride=FFT_L), :]
            x0 = x0_ref[pl.ds(m0 + i, hh, stride=FFT_L), :]
            o_ref[pl.ds(m0 + i, hh, stride=FFT_L), :] = (y + dvec * zt) * x0
        return carry

    lax.fori_loop(0, FFT_L // gl, last, 0)


def _zfft(z, x0c, kr, ki, kn, d_bias, tabs, nl):
    big = 2 * nl
    h = big // FFT_L
    l2 = FFT_L // 2
    names = ["fhr", "fhi", "flr", "fli", "twr", "twi", "clr", "cli", "chr", "chi", "ctr", "cti"]
    one = pl.Buffered(1)
    col = lambda rows: pl.BlockSpec((rows, LANES), lambda j: (0, j), pipeline_mode=one)
    vec = pl.BlockSpec((1, LANES), lambda j: (0, j))
    return pl.pallas_call(
        functools.partial(_zfft_kernel, h=h),
        grid=(BW // LANES,),
        in_specs=[col(nl), col(nl), col(h * l2), col(h * l2), vec, vec] + _table_specs(tabs, names),
        out_specs=pl.BlockSpec((nl, LANES), lambda j: (0, j)),
        out_shape=jax.ShapeDtypeStruct((nl, BW), F32),
        scratch_shapes=[pltpu.VMEM((h * FFT_PITCH, LANES), F32)] * 2,
        compiler_params=_cp(("parallel",)),
        name="hyena_fftconv",
    )(z, x0c, kr, ki, kn, d_bias.reshape(1, BW), *[tabs[k] for k in names])


def _hsmall_kernel(z_ref, x0_ref, k_ref, d_ref, fr_ref, fi_ref, o_ref, *, nc):
    big = 2 * nc
    fr = fr_ref[...].astype(BF16)
    fi = fi_ref[...].astype(BF16)
    z = z_ref[...]
    zb = z.astype(BF16)
    kb = k_ref[...].astype(BF16)
    xr = _dot(fr[:, :nc], zb)
    xi = _dot(fi[:, :nc], zb)
    kr = _dot(fr, kb)
    ki = _dot(fi, kb)
    pr = (xr * kr - xi * ki).astype(BF16)
    pi = (xr * ki + xi * kr).astype(BF16)
    y = (_dot(fr[:nc, :], pr) + _dot(fi[:nc, :], pi)) * (1.0 / big)
    o_ref[...] = (y + d_ref[...] * z) * x0_ref[...]


def _hsmall(z, x0c, kt_c, d_bias, nl, nc):
    big = 2 * nc
    ang = 2.0 * np.pi * np.outer(np.arange(big), np.arange(big)) / big
    fr = jnp.asarray(np.cos(ang), F32)
    fi = jnp.asarray(-np.sin(ang), F32)
    tc = 256
    rb = nl // nc
    return pl.pallas_call(
        functools.partial(_hsmall_kernel, nc=nc),
        grid=(BW // tc,),
        in_specs=[pl.BlockSpec((nc, tc), lambda j: (rb, j)),
                  pl.BlockSpec((nc, tc), lambda j: (rb, j)),
                  pl.BlockSpec((big, tc), lambda j: (0, j)),
                  pl.BlockSpec((1, tc), lambda j: (0, j)),
                  pl.BlockSpec((big, big), lambda j: (0, 0)),
                  pl.BlockSpec((big, big), lambda j: (0, 0))],
        out_specs=pl.BlockSpec((nc, tc), lambda j: (0, j)),
        out_shape=jax.ShapeDtypeStruct((nc, BW), F32),
        compiler_params=_cp(("parallel",)),
        name="hyena_ctx",
    )(z, x0c, kt_c, d_bias.reshape(1, BW), fr, fi)


def _merge_kernel(ya, yb, ycl, ycc, yd, g0, g1, g2, g3, wb_ref, o_ref):
    is_ctx = pl.program_id(1) == pl.num_programs(1) - 1
    yc = jnp.where(is_ctx, ycc[...], ycl[...])
    acc = None
    for b, (y, g) in enumerate(((ya[...], g0), (yb[...], g1), (yc, g2), (yd[...], g3))):
        p = _dot(y.astype(BF16), wb_ref[b].astype(BF16))
        t = jax.nn.sigmoid(g[...]) * p
        acc = t if acc is None else acc + t
    o_ref[...] = acc.astype(o_ref.dtype)


def _merge(ya, yb, yc_lat, yc_ctx, yd, pc, w_branch, l, gcol, nc):
    nt = ya.shape[0]
    tn = 512
    d = D_MODEL
    nb_lat = yc_lat.shape[0] // nc
    ysp = pl.BlockSpec((nc, BW), lambda j, i: (i, 0))
    gsp = [pl.BlockSpec((nc, tn), lambda j, i, b=b: (i, (gcol + b * d) // tn + j)) for b in range(4)]
    return pl.pallas_call(
        _merge_kernel,
        grid=(d // tn, nt // nc),
        in_specs=[ysp, ysp,
                  pl.BlockSpec((nc, BW), lambda j, i: (jnp.minimum(i, nb_lat - 1), 0)),
                  pl.BlockSpec((nc, BW), lambda j, i: (0, 0)),
                  ysp] + gsp + [pl.BlockSpec((None, 4, BW, tn), lambda j, i: (l, 0, 0, j))],
        out_specs=pl.BlockSpec((nc, tn), lambda j, i: (i, j)),
        out_shape=jax.ShapeDtypeStruct((nt, d), BF16),
        compiler_params=_cp(("parallel", "arbitrary")),
        name="merge",
    )(ya, yb, yc_lat, yc_ctx, yd, pc, pc, pc, pc, w_branch)


def _post_norm(x, delta, gate, lng, lnb):
    y = ALPHA * x + gate * delta
    mu = jnp.mean(y, axis=-1, keepdims=True)
    yc = y - mu
    var = jnp.mean(yc * yc, axis=-1, keepdims=True)
    return yc * lax.rsqrt(var + LN_EPS) * lng + lnb


def _ln1_kernel(x_ref, dl_ref, g_ref, lng_ref, lnb_ref, sh_ref, sc_ref, wr_ref, xo_ref, uo_ref, lg_ref):
    is_ctx = pl.program_id(0) == pl.num_programs(0) - 1
    xn = _post_norm(x_ref[...], dl_ref[...], _pick_mod(g_ref[...], is_ctx), lng_ref[...], lnb_ref[...])
    xo_ref[...] = xn
    u = xn * (1.0 + _pick_mod(sc_ref[...], is_ctx)) + _pick_mod(sh_ref[...], is_ctx)
    uo_ref[...] = u
    lg_ref[...] = _dot_nt_hi(wr_ref[...], u)


def _ln1(x, mix, mod_l, lng, lnb, w_router, nc):
    nt, d = x.shape
    row = pl.BlockSpec((nc, d), lambda i: (i, 0))
    modc = lambda c: pl.BlockSpec((8, d), lambda i: (0, c))
    vec = pl.BlockSpec((1, d), lambda i: (0, 0))
    return pl.pallas_call(
        _ln1_kernel,
        grid=(nt // nc,),
        in_specs=[row, row, modc(2), vec, vec, modc(3), modc(4),
                  pl.BlockSpec((N_EXPERTS, d), lambda i: (0, 0))],
        out_specs=[row, row, pl.BlockSpec((N_EXPERTS, nc), lambda i: (0, i))],
        out_shape=[jax.ShapeDtypeStruct((nt, d), F32), jax.ShapeDtypeStruct((nt, d), F32),
                   jax.ShapeDtypeStruct((N_EXPERTS, nt), F32)],
        compiler_params=_cp(("parallel",)),
        name="ln1_router",
    )(x, mix, mod_l, lng.reshape(1, d), lnb.reshape(1, d), mod_l, mod_l, w_router.T)


def _cumsum_lanes(x):
    n = x.shape[1]
    lane = lax.broadcasted_iota(I32, x.shape, 1)
    s = 1
    while s < n:
        x = x + jnp.where(lane >= s, pltpu.roll(x, s, 1), 0.0)
        s *= 2
    return x


def _select_kernel(lg_ref, idx_ref, w_ref, posx_ref, posm_ref, aff_scr, pos_scr, *, n, cap, sc):
    lg = lg_ref[...]
    e = jnp.exp(lg - jnp.max(lg, axis=0, keepdims=True))
    aff = e / jnp.sum(e, axis=0, keepdims=True)
    bits = pltpu.bitcast(aff, I32)
    tau = jnp.zeros((N_EXPERTS, 1), I32)
    for bit in range(30, -1, -1):
        cand = tau | (1 << bit)
        cnt = jnp.sum((bits >= cand).astype(F32), axis=1, keepdims=True)
        tau = jnp.where(cnt >= cap, cand, tau)
    gt = bits > tau
    eq = (bits == tau).astype(F32)
    need = cap - jnp.sum(gt.astype(F32), axis=1, keepdims=True)
    eq_rank = _cumsum_lanes(eq) - eq
    sel = jnp.where(gt, 1.0, jnp.where(eq_rank < need, eq, 0.0))
    posx = _cumsum_lanes(sel) - sel
    posx_ref[...] = posx.astype(I32)
    posm = jnp.where(sel > 0.0, posx, -1.0)
    posm_ref[...] = posm.astype(I32)
    aff_scr[...] = aff
    pos_scr[...] = posm
    nch = cap // sc
    tok = lax.broadcasted_iota(I32, (sc, n), 1).astype(F32)
    slot0 = lax.broadcasted_iota(I32, (sc, 1), 0).astype(F32)
    lane = lax.broadcasted_iota(I32, (sc, LANES), 1)

    def per_expert(ex, carry):
        it, wt = carry
        prow = pos_scr[pl.ds(ex, 1), :]
        arow = aff_scr[pl.ds(ex, 1), :]
        for ch in range(nch):
            hit = prow == (slot0 + float(ch * sc))
            icol = jnp.sum(jnp.where(hit, tok, 0.0), axis=1, keepdims=True)
            wcol = jnp.sum(jnp.where(hit, arow, 0.0), axis=1, keepdims=True)
            here = lane == ex * nch + ch
            it = jnp.where(here, icol, it)
            wt = jnp.where(here, wcol, wt)
        return it, wt

    zero = jnp.zeros((sc, LANES), F32)
    it, wt = lax.fori_loop(0, N_EXPERTS, per_expert, (zero, zero))
    idx_ref[...] = it.astype(I32)
    w_ref[...] = wt


def _select(lg_t, col0, n, cap):
    sc = min(LANES, cap)
    nch = cap // sc
    assert nch * N_EXPERTS <= LANES and col0 % n == 0
    idx_t, w_t, posx, posm = pl.pallas_call(
        functools.partial(_select_kernel, n=n, cap=cap, sc=sc),
        grid=(1,),
        in_specs=[pl.BlockSpec((N_EXPERTS, n), lambda i: (0, col0 // n))],
        out_specs=[pl.BlockSpec((sc, LANES), lambda i: (0, 0)), pl.BlockSpec((sc, LANES), lambda i: (0, 0)),
                   pl.BlockSpec((N_EXPERTS, n), lambda i: (0, 0)), pl.BlockSpec((N_EXPERTS, n), lambda i: (0, 0))],
        out_shape=[jax.ShapeDtypeStruct((sc, LANES), I32), jax.ShapeDtypeStruct((sc, LANES), F32),
                   jax.ShapeDtypeStruct((N_EXPERTS, n), I32), jax.ShapeDtypeStruct((N_EXPERTS, n), I32)],
        scratch_shapes=[pltpu.VMEM((N_EXPERTS, n), F32)] * 2,
        compiler_params=_cp(("arbitrary",)),
        name="moe_select",
    )(lg_t)
    unt = lambda a: a[:, :N_EXPERTS * nch].T.reshape(N_EXPERTS, cap)
    return unt(idx_t), unt(w_t), posx, posm


def _ffn_kernel(idx_ref, u_hbm, ws_ref, w1_ref, w3_ref, w2_ref, o_ref, xg, xb, acc, sem, *, capt, capp):
    ex = pl.program_id(0)
    f = pl.program_id(1)

    def row_copy(r, src_row):
        return pltpu.make_async_copy(u_hbm.at[pl.ds(src_row, 1)], xg.at[pl.ds(r, 1)], sem)

    def gather(which):
        def issue(r, carry):
            row_copy(r, idx_ref[which, r]).start()
            return carry

        lax.fori_loop(0, capt, issue, 0, unroll=8)

    @pl.when(f == 0)
    def _():
        @pl.when(ex == 0)
        def _():
            gather(ex)

        def drain(r, carry):
            row_copy(r, 0).wait()
            return carry

        lax.fori_loop(0, capt, drain, 0, unroll=8)
        xb[...] = xg[...].astype(BF16)

        @pl.when(ex + 1 < pl.num_programs(0))
        def _():
            gather(ex + 1)

        acc[...] = jnp.zeros_like(acc)

    x = xb[...]
    hdn = _silu(_dot(x, w1_ref[...].astype(BF16))) * _dot(x, w3_ref[...].astype(BF16))
    acc[...] += _dot(hdn.astype(BF16), w2_ref[...].astype(BF16))

    @pl.when(f == pl.num_programs(1) - 1)
    def _():
        o_ref[0:capt, :] = (acc[...] * ws_ref[...]).astype(o_ref.dtype)
        o_ref[capt:capp, :] = jnp.zeros((capp - capt, o_ref.shape[1]), o_ref.dtype)


def _ffn(idx, u2, wsel, w_e1, w_e3, w_e2, l, capp):
    capt = idx.shape[1]
    d = D_MODEL
    tf = 256
    grid_spec = pltpu.PrefetchScalarGridSpec(
        num_scalar_prefetch=1,
        grid=(N_EXPERTS, EXPERT_FF // tf),
        in_specs=[pl.BlockSpec(memory_space=pl.ANY),
                  pl.BlockSpec((None, capt, 1), lambda e, f, idx: (e, 0, 0)),
                  pl.BlockSpec((None, None, d, tf), lambda e, f, idx: (l, e, 0, f)),
                  pl.BlockSpec((None, None, d, tf), lambda e, f, idx: (l, e, 0, f)),
                  pl.BlockSpec((None, None, tf, d), lambda e, f, idx: (l, e, f, 0))],
        out_specs=pl.BlockSpec((None, capp, d), lambda e, f, idx: (e, 0, 0)),
        scratch_shapes=[pltpu.VMEM((capt, d), F32), pltpu.VMEM((capt, d), BF16), pltpu.VMEM((capt, d), F32),
                        pltpu.SemaphoreType.DMA(())],
    )
    return pl.pallas_call(
        functools.partial(_ffn_kernel, capt=capt, capp=capp),
        grid_spec=grid_spec,
        out_shape=jax.ShapeDtypeStruct((N_EXPERTS, capp, d), BF16),
        compiler_params=_cp(("arbitrary", "arbitrary")),
        name="moe_ffn",
    )(idx, u2, wsel[..., None], w_e1, w_e3, w_e2)


COMB_CHUNK = 64
COMB_ALIGN = 16


def _comb_chunks(nc):
    return -(-(nc + COMB_ALIGN - 1) // COMB_CHUNK)


def _combine_kernel(st_ref, nch_ref, posm_ref, y_hbm, x_ref, g_ref, lng_ref, lnb_ref, sh_ref, sc_ref,
                    xo_ref, uo_ref, ywin, acc_ref, sem, *, max_ch):
    b = pl.program_id(0)
    is_ctx = b == pl.num_programs(0) - 1
    nc = x_ref.shape[0]
    ch = COMB_CHUNK

    def chunk_copy(ex, base, j):
        return pltpu.make_async_copy(y_hbm.at[ex, pl.ds(base + j * ch, ch)], ywin.at[ex * max_ch + j], sem.at[ex])

    bases = []
    for ex in range(N_EXPERTS):
        s0 = st_ref[ex, b]
        base = pl.multiple_of(s0 - (s0 & (COMB_ALIGN - 1)), COMB_ALIGN)
        bases.append(base)

        def issue(j, carry, ex=ex, base=base):
            chunk_copy(ex, base, j).start()
            return carry

        lax.fori_loop(0, nch_ref[ex, b], issue, 0)
    acc_ref[...] = jnp.zeros_like(acc_ref)
    posm = posm_ref[...]
    slot = lax.broadcasted_iota(I32, (ch, nc), 0)
    for ex in range(N_EXPERTS):
        rel = posm[ex:ex + 1, :] - bases[ex]

        def work(j, carry, ex=ex, base=bases[ex], rel=rel):
            chunk_copy(ex, base, j).wait()
            hit = (rel - j * ch) == slot
            acc_ref[...] += _dot_tn(hit.astype(BF16), ywin[ex * max_ch + j])
            return carry

        lax.fori_loop(0, nch_ref[ex, b], work, 0)
    xn = _post_norm(x_ref[...], acc_ref[...], _pick_mod(g_ref[...], is_ctx), lng_ref[...], lnb_ref[...])
    xo_ref[...] = xn
    u = xn * (1.0 + _pick_mod(sc_ref[...], is_ctx)) + _pick_mod(sh_ref[...], is_ctx)
    uo_ref[...] = u.astype(uo_ref.dtype)


def _combine(starts, counts, posm, y, x1, mod_l, lng, lnb, mod_next, nc):
    nt, d = x1.shape
    max_ch = _comb_chunks(nc)
    nch = (((starts & (COMB_ALIGN - 1)) + counts + COMB_CHUNK - 1) // COMB_CHUNK).astype(I32)
    row = lambda: pl.BlockSpec((nc, d), lambda i, st, nq: (i, 0))
    modc = lambda c: pl.BlockSpec((8, d), lambda i, st, nq: (0, c))
    vec = pl.BlockSpec((1, d), lambda i, st, nq: (0, 0))
    grid_spec = pltpu.PrefetchScalarGridSpec(
        num_scalar_prefetch=2,
        grid=(nt // nc,),
        in_specs=[pl.BlockSpec((N_EXPERTS, nc), lambda i, st, nq: (0, i)),
                  pl.BlockSpec(memory_space=pl.ANY),
                  row(), modc(5), vec, vec, modc(0), modc(1)],
        out_specs=[row(), row()],
        scratch_shapes=[pltpu.VMEM((N_EXPERTS * max_ch, COMB_CHUNK, d), BF16), pltpu.VMEM((nc, d), F32),
                        pltpu.SemaphoreType.DMA((N_EXPERTS,))],
    )
    return pl.pallas_call(
        functools.partial(_combine_kernel, max_ch=max_ch),
        grid_spec=grid_spec,
        out_shape=[jax.ShapeDtypeStruct((nt, d), F32), jax.ShapeDtypeStruct((nt, d), BF16)],
        compiler_params=_cp(("arbitrary",)),
        name="moe_combine_ln2",
    )(starts, nch, posm, y, x1, mod_l, lng.reshape(1, d), lnb.reshape(1, d), mod_next, mod_next)


def _row_tile(nt, nc):
    return nt // 4 if (nt % 4 == 0 and (nt // 4) % 16 == 0 and nt // 4 >= nc) else nc


def kernel(x, c, ctx, c_ctx, w_mod, b_mod, w_in, attn_sink, delta_conv, delta_a_log, delta_dt_bias, delta_norm_g, hyena_conv, hyena_w1, hyena_b1, hyena_freq, hyena_w2, hyena_b2, hyena_w3, hyena_b3, hyena_w4, hyena_b4, hyena_d, na_rel_bias, w_branch, w_o, ln1_g, ln1_b, ln2_g, ln2_b, w_router, w_e1, w_e3, w_e2):
    nl, nc = x.shape[1], ctx.shape[1]
    nt = nl + nc
    depth = w_in.shape[0]
    tm = _row_tile(nt, nc)
    cap_l = EC_CAPACITY * nl // N_EXPERTS
    cap_c = EC_CAPACITY * nc // N_EXPERTS
    capp = cap_l + cap_c + _comb_chunks(nc) * COMB_CHUNK

    xc = jnp.concatenate([x[0], ctx[0]], axis=0)
    mod = _modvec(c, c_ctx, w_mod, b_mod)
    cos_t, sin_t = _rope_tables(nl, nc)
    tabs = _fft_tables(nl)
    zk_l, absd = _hfilter_features(nl)
    zk_c, _ = _hfilter_features(nc)
    u = _modulate(xc, mod[0], nc)

    for l in range(depth):
        pa = _matmul(u, w_in, l, 0, SEG1, tm, 512, name="in_proj_a")
        pab = _matmul(u, w_in[l, :, SEG1:SEG1 + SEG_AB][None], 0, 0, SEG_AB, tm, SEG_AB, name="in_proj_ab")
        pc = _matmul(u, w_in[l, :, SEG1 + SEG_AB:][None], 0, 0, SEG3, tm, 512, name="in_proj_c")
        ya = _attn_a(_rope(pa, cos_t, sin_t, nc), attn_sink[l], nl, nc)
        qkvb = _dprep(pa, delta_conv[l], nl, nc, A_HEADS * HEAD_DIM + 2 * A_KV * HEAD_DIM)
        o_f, o_b = _delta(qkvb, pab, delta_a_log[l], delta_dt_bias[l], nl, nc)
        yb = _dpost(o_f, o_b, pa, delta_norm_g[l], nc, SEG1 - BW)
        filt = (hyena_w1[l], hyena_b1[l], hyena_freq[l], hyena_w2[l], hyena_b2[l], hyena_w3[l], hyena_b3[l],
                hyena_w4[l], hyena_b4[l])
        x0c, z = _hconv(pc, hyena_conv[l], nl, nc)
        kr, ki, kn = _kfft(_hfilter(nl, zk_l, absd, *filt), tabs, nl)
        yc_lat = _zfft(z, x0c, kr, ki, kn, hyena_d[l], tabs, nl)
        yc_ctx = _hsmall(z, x0c, _hfilter(nc, zk_c, absd, *filt), hyena_d[l], nl, nc)
        yd = _na(pc, _na_bias_table(na_rel_bias[l]), nl, nc, 3 * BW)
        mrg = _merge(ya, yb, yc_lat, yc_ctx, yd, pc, w_branch, l, 6 * BW, nc)
        mix = _matmul(mrg, w_o, l, 0, D_MODEL, tm, 512, name="out_proj")
        x1, u2, lg_t = _ln1(xc, mix, mod[l], ln1_g[l], ln1_b[l], w_router[l], nc)
        idx_l, ws_l, posx_l, posm_l = _select(lg_t, 0, nl, cap_l)
        idx_c, ws_c, posx_c, posm_c = _select(lg_t, nl, nc, cap_c)
        idx = jnp.concatenate([idx_l, idx_c + nl], axis=1)
        wsel = jnp.concatenate([ws_l, ws_c], axis=1)
        y = _ffn(idx, u2, wsel, w_e1, w_e3, w_e2, l, capp)
        posm = jnp.concatenate([posm_l, jnp.where(posm_c >= 0, posm_c + cap_l, -1)], axis=1)
        st_l = posx_l[:, ::nc]
        cnt_l = jnp.concatenate([st_l[:, 1:], jnp.full((N_EXPERTS, 1), cap_l, I32)], axis=1) - st_l
        starts = jnp.concatenate([st_l, posx_c[:, :1] + cap_l], axis=1)
        counts = jnp.concatenate([cnt_l, jnp.full((N_EXPERTS, 1), cap_c, I32)], axis=1)
        xc, u = _combine(starts, counts, posm, y, x1, mod[l], ln2_g[l], ln2_b[l], mod[min(l + 1, depth - 1)], nc)
    return xc[:nl][None]
```

```python
import functools
import math

import numpy as np
import jax
import jax.numpy as jnp
from jax import lax
from jax.experimental import pallas as pl
from jax.experimental.pallas import tpu as pltpu

F32 = jnp.float32
BF16 = jnp.bfloat16
I32 = jnp.int32
HI = lax.Precision.HIGHEST

D_MODEL = 2048
GRID_W = 64
HEAD_DIM = 128
LANES = 128
BW = D_MODEL // 2
A_HEADS, A_KV = 8, 2
A_WINDOW = 128
A_BLK = 128
B_HEADS = 8
B_CHUNK = 64
D_HEADS = 8
NA_KH, NA_KW = 8, 16
N_EXPERTS = 16
EXPERT_FF = D_MODEL // 2
EC_CAPACITY = 2
HYENA_EMB, HYENA_FFN = 33, 64
ROPE_BASE = 10000.0
LN_EPS = 1e-6
NORM_EPS = 1e-6
NEG_INF = -1e30
DEPTH = 4
ALPHA = (2.0 * DEPTH) ** 0.25

SEG1 = 5632
SEG_AB = 32
SEG3 = 14336
FFT_L = 128

VMEM_LIMIT = 56 * 1024 * 1024


def _cp(sem=None, vmem=VMEM_LIMIT):
    kw = dict(vmem_limit_bytes=vmem)
    if sem is not None:
        kw["dimension_semantics"] = sem
    return pltpu.CompilerParams(**kw)


def _dot(a, b):
    return jnp.dot(a, b, preferred_element_type=F32)


def _dot_hi(a, b):
    return jnp.dot(a, b, preferred_element_type=F32, precision=HI)


def _dot_nt(a, b):
    return lax.dot_general(a, b, (((1,), (1,)), ((), ())), preferred_element_type=F32)


def _dot_nt_hi(a, b):
    return lax.dot_general(a, b, (((1,), (1,)), ((), ())), preferred_element_type=F32, precision=HI)


def _dot_tn(a, b):
    return lax.dot_general(a, b, (((0,), (0,)), ((), ())), preferred_element_type=F32)


def _silu(x):
    return x * jax.nn.sigmoid(x)


def _modvec_kernel(ct_ref, w_ref, b_ref, o_ref):
    s = _silu(ct_ref[...])
    w = w_ref[...]
    r0 = jnp.sum(w * s[:, 0:1], axis=0, keepdims=True) + b_ref[...]
    r1 = jnp.sum(w * s[:, 1:2], axis=0, keepdims=True) + b_ref[...]
    o_ref[...] = jnp.concatenate([r0, r1, jnp.zeros((6, w.shape[1]), F32)], axis=0)


def _modvec(c, c_ctx, w_mod, b_mod):
    depth, d, n6 = w_mod.shape
    tn = 512
    ct = jnp.zeros((d, LANES), F32).at[:, 0].set(c[0]).at[:, 1].set(c_ctx)
    return pl.pallas_call(
        _modvec_kernel,
        grid=(depth, n6 // tn),
        in_specs=[pl.BlockSpec((d, LANES), lambda l, j: (0, 0)),
                  pl.BlockSpec((None, d, tn), lambda l, j: (l, 0, j)),
                  pl.BlockSpec((None, 1, tn), lambda l, j: (l, 0, j))],
        out_specs=pl.BlockSpec((None, 8, tn), lambda l, j: (l, 0, j)),
        out_shape=jax.ShapeDtypeStruct((depth, 8, n6), F32),
        compiler_params=_cp(("parallel", "parallel")),
        name="modvec",
    )(ct, w_mod, b_mod.reshape(depth, 1, n6))


def _pick_mod(m, is_ctx):
    return jnp.where(is_ctx, m[1:2, :], m[0:1, :])


def _modulate_kernel(x_ref, sh_ref, sc_ref, o_ref):
    is_ctx = pl.program_id(0) == pl.num_programs(0) - 1
    sh = _pick_mod(sh_ref[...], is_ctx)
    sc = _pick_mod(sc_ref[...], is_ctx)
    o_ref[...] = (x_ref[...] * (1.0 + sc) + sh).astype(o_ref.dtype)


def _modulate(x, mod_l, nc):
    nt, d = x.shape
    return pl.pallas_call(
        _modulate_kernel,
        grid=(nt // nc,),
        in_specs=[pl.BlockSpec((nc, d), lambda i: (i, 0)),
                  pl.BlockSpec((8, d), lambda i: (0, 0)),
                  pl.BlockSpec((8, d), lambda i: (0, 1))],
        out_specs=pl.BlockSpec((nc, d), lambda i: (i, 0)),
        out_shape=jax.ShapeDtypeStruct((nt, d), BF16),
        compiler_params=_cp(("parallel",)),
        name="modulate",
    )(x, mod_l, mod_l)


def _mm_kernel(a_ref, w_ref, o_ref):
    o_ref[...] = _dot(a_ref[...], w_ref[...].astype(BF16)).astype(o_ref.dtype)


def _matmul(a, w, l, col0, n, tm, tn, out_dtype=F32, name="mm"):
    m, k = a.shape
    assert m % tm == 0 and n % tn == 0 and col0 % tn == 0
    cb = col0 // tn
    return pl.pallas_call(
        _mm_kernel,
        grid=(m // tm, n // tn),
        in_specs=[pl.BlockSpec((tm, k), lambda i, j: (i, 0)),
                  pl.BlockSpec((None, k, tn), lambda i, j: (l, 0, cb + j))],
        out_specs=pl.BlockSpec((tm, tn), lambda i, j: (i, j)),
        out_shape=jax.ShapeDtypeStruct((m, n), out_dtype),
        compiler_params=_cp(("parallel", "parallel")),
        name=name,
    )(a, w)


def _mm_shift_kernel(a_ref, w_ref, wt_ref, o_ref, *, shift):
    w = jnp.concatenate([w_ref[:, shift:].astype(BF16), wt_ref[:, :shift].astype(BF16)], axis=1)
    o_ref[...] = _dot(a_ref[...], w).astype(o_ref.dtype)


def _matmul_shifted(a, w, l, col0, shift, n, tm, tn, out_dtype=F32, name="mm_shift"):
    m, k = a.shape
    assert m % tm == 0 and n % tn == 0 and col0 % tn == 0 and 0 < shift < LANES
    cb = col0 // tn
    tpb = tn // LANES
    return pl.pallas_call(
        functools.partial(_mm_shift_kernel, shift=shift),
        grid=(m // tm, n // tn),
        in_specs=[pl.BlockSpec((tm, k), lambda i, j: (i, 0)),
                  pl.BlockSpec((None, k, tn), lambda i, j: (l, 0, cb + j)),
                  pl.BlockSpec((None, k, LANES), lambda i, j: (l, 0, (cb + j + 1) * tpb))],
        out_specs=pl.BlockSpec((tm, tn), lambda i, j: (i, j)),
        out_shape=jax.ShapeDtypeStruct((m, n), out_dtype),
        compiler_params=_cp(("parallel", "parallel")),
        name=name,
    )(a, w, w)


def _rope_tables(nl, nc):
    half = HEAD_DIM // 2
    nf = half // 2
    inv_freq = ROPE_BASE ** (-jnp.arange(nf, dtype=F32) / nf)
    tpos = jnp.arange(nl)
    ang_r = (tpos // GRID_W).astype(F32)[:, None] * inv_freq[None, :]
    ang_c = (tpos % GRID_W).astype(F32)[:, None] * inv_freq[None, :]
    cr, sr, cc, sc = jnp.cos(ang_r), jnp.sin(ang_r), jnp.cos(ang_c), jnp.sin(ang_c)
    cos_t = jnp.concatenate([cr, cr, cc, cc], axis=1)
    sin_t = jnp.concatenate([-sr, sr, -sc, sc], axis=1)
    cos_t = jnp.concatenate([cos_t, jnp.ones((nc, HEAD_DIM), F32)], axis=0)
    sin_t = jnp.concatenate([sin_t, jnp.zeros((nc, HEAD_DIM), F32)], axis=0)
    return cos_t, sin_t


def _rope_kernel(x_ref, c_ref, s_ref, o_ref, *, n_rot, n_sl):
    cos_t = c_ref[...]
    sin_t = s_ref[...]
    lane = lax.broadcasted_iota(I32, cos_t.shape, 1)
    quarter = HEAD_DIM // 4
    first = (lane % (2 * quarter)) < quarter
    for j in range(n_sl):
        sl = slice(j * HEAD_DIM, (j + 1) * HEAD_DIM)
        x = x_ref[:, sl]
        if j < n_rot:
            swapped = jnp.where(first, pltpu.roll(x, HEAD_DIM - quarter, 1), pltpu.roll(x, quarter, 1))
            x = x * cos_t + swapped * sin_t
        o_ref[:, sl] = x.astype(o_ref.dtype)


def _rope(pa, cos_t, sin_t, tr):
    nt = pa.shape[0]
    n_rot = A_HEADS + A_KV
    n_sl = A_HEADS + 2 * A_KV
    width = n_sl * HEAD_DIM
    return pl.pallas_call(
        functools.partial(_rope_kernel, n_rot=n_rot, n_sl=n_sl),
        grid=(nt // tr,),
        in_specs=[pl.BlockSpec((tr, width), lambda i: (i, 0)),
                  pl.BlockSpec((tr, HEAD_DIM), lambda i: (i, 0)),
                  pl.BlockSpec((tr, HEAD_DIM), lambda i: (i, 0))],
        out_specs=pl.BlockSpec((tr, width), lambda i: (i, 0)),
        out_shape=jax.ShapeDtypeStruct((nt, width), BF16),
        compiler_params=_cp(("parallel",)),
        name="rope",
    )(pa, cos_t, sin_t)


def _attn_a_kernel(sink_ref, q_ref, k_ref, v_ref, o_ref, *, nl, nc):
    h = pl.program_id(0)
    qb = pl.program_id(1)
    grp = A_HEADS // A_KV
    scale = HEAD_DIM ** -0.5
    q = q_ref[...]
    qs = jnp.concatenate([q[:, g * HEAD_DIM:(g + 1) * HEAD_DIM] for g in range(grp)], axis=0)
    rows = grp * A_BLK
    wlen = 3 * A_BLK
    is_lat = qb < nl // A_BLK
    ws = pl.multiple_of(jnp.clip((qb - 1) * A_BLK, 0, nl - wlen), A_BLK)
    kc = k_ref[nl:nl + nc, :]
    vc = v_ref[nl:nl + nc, :]
    kw = k_ref[pl.ds(ws, wlen), :]
    vw = v_ref[pl.ds(ws, wlen), :]
    s_c = _dot_nt(qs, kc) * scale
    s_l = _dot_nt(qs, kw) * scale
    ri = lax.broadcasted_iota(I32, (rows, wlen), 0)
    ci = lax.broadcasted_iota(I32, (rows, wlen), 1)
    qabs = qb * A_BLK + (ri % A_BLK)
    kabs = ws + ci
    mask = is_lat & (jnp.abs(kabs - qabs) <= A_WINDOW)
    s_l = jnp.where(mask, s_l, NEG_INF)
    rcol = lax.broadcasted_iota(I32, (rows, 1), 0) // A_BLK
    sink = jnp.zeros((rows, 1), F32)
    for g in range(grp):
        sink = jnp.where(rcol == g, sink_ref[h * grp + g], sink)
    m = jnp.maximum(jnp.maximum(jnp.max(s_c, axis=-1, keepdims=True),
                                jnp.max(s_l, axis=-1, keepdims=True)), sink)
    p_c = jnp.exp(s_c - m)
    p_l = jnp.exp(s_l - m)
    den = (jnp.sum(p_c, axis=-1, keepdims=True) + jnp.sum(p_l, axis=-1, keepdims=True)
           + jnp.exp(sink - m))
    y = (_dot(p_c.astype(BF16), vc) + _dot(p_l.astype(BF16), vw)) / den
    for g in range(grp):
        o_ref[:, g * HEAD_DIM:(g + 1) * HEAD_DIM] = y[g * A_BLK:(g + 1) * A_BLK].astype(o_ref.dtype)


def _attn_a(qkv, sink, nl, nc):
    nt = qkv.shape[0]
    grp = A_HEADS // A_KV
    qw = grp * HEAD_DIM
    kcol = A_HEADS
    vcol = A_HEADS + A_KV
    return pl.pallas_call(
        functools.partial(_attn_a_kernel, nl=nl, nc=nc),
        grid=(A_KV, nt // A_BLK),
        in_specs=[pl.BlockSpec(memory_space=pltpu.SMEM),
                  pl.BlockSpec((A_BLK, qw), lambda h, b: (b, h)),
                  pl.BlockSpec((nt, HEAD_DIM), lambda h, b: (0, kcol + h)),
                  pl.BlockSpec((nt, HEAD_DIM), lambda h, b: (0, vcol + h))],
        out_specs=pl.BlockSpec((A_BLK, qw), lambda h, b: (b, h)),
        out_shape=jax.ShapeDtypeStruct((nt, A_HEADS * HEAD_DIM), BF16),
        compiler_params=_cp(("parallel", "arbitrary")),
        name="attn_a",
    )(sink, qkv, qkv, qkv)


def _na_bias_table(rel_bias):
    cq = jnp.arange(GRID_W)
    col_start = jnp.clip(cq - NA_KW // 2, 0, GRID_W - NA_KW)
    col_ok = (cq[None, :] >= col_start[:, None]) & (cq[None, :] < col_start[:, None] + NA_KW)
    dc = jnp.clip(cq[None, :] - cq[:, None], -(NA_KW - 1), NA_KW - 1) + (NA_KW - 1)
    b = rel_bias.astype(F32)[:, :, dc]
    b = jnp.where(col_ok[None, None], b, NEG_INF)
    rows = [jnp.concatenate([b[:, d0 + j] for j in range(NA_KH)], axis=-1) for d0 in range(NA_KH)]
    return jnp.stack(rows, axis=1)


def _na_kernel(q_ref, k_ref, v_ref, bw_ref, o_ref, *, nl, nc):
    qb = pl.program_id(1)
    scale = HEAD_DIM ** -0.5
    rows = nl // GRID_W
    rpb = nc // GRID_W
    kc = k_ref[nl:nl + nc, :].astype(BF16)
    vc = v_ref[nl:nl + nc, :].astype(BF16)

    @pl.when(qb == nl // nc)
    def _():
        q = q_ref[...].astype(BF16)
        s = _dot_nt(q, kc) * scale
        m = jnp.max(s, axis=-1, keepdims=True)
        p = jnp.exp(s - m)
        den = jnp.sum(p, axis=-1, keepdims=True)
        o_ref[...] = (_dot(p.astype(BF16), vc) / den).astype(o_ref.dtype)

    @pl.when(qb < nl // nc)
    def _():
        q_all = q_ref[...].astype(BF16)
        s_c_all = _dot_nt(q_all, kc) * scale
        rows_ = []
        for i in range(rpb):
            r = qb * rpb + i
            st = jnp.clip(r - NA_KH // 2, 0, rows - NA_KH)
            w0 = pl.multiple_of(st * GRID_W, GRID_W)
            rows_.append(dict(
                q=q_all[i * GRID_W:(i + 1) * GRID_W], s_c=s_c_all[i * GRID_W:(i + 1) * GRID_W],
                kw=k_ref[pl.ds(w0, NA_KH * GRID_W), :].astype(BF16),
                vw=v_ref[pl.ds(w0, NA_KH * GRID_W), :].astype(BF16),
                bias=bw_ref[st - r + (NA_KH - 1)]))
        for e in rows_:
            e["s_l"] = _dot_nt(e["q"], e["kw"]) * scale + e["bias"]
        for e in rows_:
            m = jnp.maximum(jnp.max(e["s_c"], axis=-1, keepdims=True), jnp.max(e["s_l"], axis=-1, keepdims=True))
            p_c = jnp.exp(e["s_c"] - m)
            p_l = jnp.exp(e["s_l"] - m)
            e["den"] = jnp.sum(p_c, axis=-1, keepdims=True) + jnp.sum(p_l, axis=-1, keepdims=True)
            e["p_c"] = p_c.astype(BF16)
            e["p_l"] = p_l.astype(BF16)
        y_c = _dot(jnp.concatenate([e["p_c"] for e in rows_], axis=0), vc)
        y_l = [_dot(e["p_l"], e["vw"]) for e in rows_]
        for i, e in enumerate(rows_):
            y = (y_c[i * GRID_W:(i + 1) * GRID_W] + y_l[i]) / e["den"]
            o_ref[i * GRID_W:(i + 1) * GRID_W, :] = y.astype(o_ref.dtype)


def _na(pc, bw, nl, nc, qcol):
    nt = pc.shape[0]
    qc = qcol // HEAD_DIM
    kcb = qc + D_HEADS
    vcb = qc + 2 * D_HEADS
    return pl.pallas_call(
        functools.partial(_na_kernel, nl=nl, nc=nc),
        grid=(D_HEADS, nt // nc),
        in_specs=[pl.BlockSpec((nc, HEAD_DIM), lambda h, b: (b, qc + h)),
                  pl.BlockSpec((nt, HEAD_DIM), lambda h, b: (0, kcb + h)),
                  pl.BlockSpec((nt, HEAD_DIM), lambda h, b: (0, vcb + h)),
                  pl.BlockSpec((None, NA_KH, GRID_W, NA_KH * GRID_W), lambda h, b: (h, 0, 0, 0))],
        out_specs=pl.BlockSpec((nc, HEAD_DIM), lambda h, b: (b, h)),
        out_shape=jax.ShapeDtypeStruct((nt, D_HEADS * HEAD_DIM), BF16),
        compiler_params=_cp(("parallel", "arbitrary")),
        name="na_attn",
    )(pc, pc, pc, bw)


def _conv3(x_ref, p_ref, n_ref, w_ref, scr, nb_lat):
    i = pl.program_id(0)
    r = x_ref.shape[0]
    x = x_ref[...]
    has_prev = (i >= 1) & (i < nb_lat)
    has_next = i < nb_lat - 1
    scr[8:8 + r, :] = x
    scr[7:8, :] = jnp.where(has_prev, p_ref[7:8, :], 0.0)
    scr[8 + r:9 + r, :] = jnp.where(has_next, n_ref[0:1, :], 0.0)
    w = w_ref[...]
    return w[0:1, :] * scr[7:7 + r, :] + w[1:2, :] * x + w[2:3, :] * scr[9:9 + r, :]


def _halo_specs(nc, tc, cb, nt):
    r8 = nc // 8
    last8 = nt // 8 - 1
    return [pl.BlockSpec((nc, tc), lambda i, j: (i, cb + j)),
            pl.BlockSpec((8, tc), lambda i, j: (jnp.maximum(i * r8 - 1, 0), cb + j)),
            pl.BlockSpec((8, tc), lambda i, j: (jnp.minimum((i + 1) * r8, last8), cb + j))]


def _dprep_kernel(x_ref, p_ref, n_ref, w_ref, o_ref, scr, *, nb_lat, n_norm):
    y = _silu(_conv3(x_ref, p_ref, n_ref, w_ref, scr, nb_lat))
    j = pl.program_id(1)
    tc = y.shape[1]
    for s in range(tc // HEAD_DIM):
        ys = y[:, s * HEAD_DIM:(s + 1) * HEAD_DIM]
        nrm = ys * lax.rsqrt(jnp.sum(ys * ys, axis=-1, keepdims=True) + NORM_EPS)
        o_ref[:, s * HEAD_DIM:(s + 1) * HEAD_DIM] = jnp.where(j < n_norm, nrm, ys)


def _dprep(pa, conv_w, nl, nc, col0):
    nt = pa.shape[0]
    tc = 512
    width = 3 * BW
    return pl.pallas_call(
        functools.partial(_dprep_kernel, nb_lat=nl // nc, n_norm=2 * BW // tc),
        grid=(nt // nc, width // tc),
        in_specs=_halo_specs(nc, tc, col0 // tc, nt) + [pl.BlockSpec((3, tc), lambda i, j: (0, j))],
        out_specs=pl.BlockSpec((nc, tc), lambda i, j: (i, j)),
        out_shape=jax.ShapeDtypeStruct((nt, width), F32),
        scratch_shapes=[pltpu.VMEM((nc + 16, tc), F32)],
        compiler_params=_cp(("parallel", "arbitrary")),
        name="delta_prep",
    )(pa, pa, pa, conv_w)


def _softplus(x):
    return jnp.maximum(x, 0.0) + jnp.log1p(jnp.exp(-jnp.abs(x)))


def _mxu(a, b, passes=1):
    ah = a.astype(BF16)
    bh = b.astype(BF16)
    out = _dot(ah, bh)
    if passes == 3:
        al = (a - ah.astype(F32)).astype(BF16)
        bl = (b - bh.astype(F32)).astype(BF16)
        out = out + _dot(al, bh) + _dot(ah, bl)
    return out


DELTA_SOLVE_PASSES = 1


def _delta_kernel(qf_ref, abf_ref, qb_ref, abb_ref, alog_ref, dtb_ref, of_ref, ob_ref, s_ref):
    ck = B_CHUNK
    nh = B_HEADS
    p = DELTA_SOLVE_PASSES

    @pl.when(pl.program_id(0) == 0)
    def _():
        s_ref[...] = jnp.zeros_like(s_ref)

    ri = lax.broadcasted_iota(I32, (ck, ck), 0)
    ci = lax.broadcasted_iota(I32, (ck, ck), 1)
    chains = []
    for d, (qkv_ref, ab_ref, o_ref) in enumerate(((qf_ref, abf_ref, of_ref), (qb_ref, abb_ref, ob_ref))):
        incl = (ri >= ci) if d == 0 else (ri <= ci)
        strict = (ri > ci) if d == 0 else (ri < ci)
        ab = ab_ref[...]
        dsl = slice(d * nh, (d + 1) * nh)
        g8 = -jnp.exp(alog_ref[:, dsl]) * _softplus(ab[:, dsl] + dtb_ref[:, dsl])
        b8 = jax.nn.sigmoid(ab[:, 2 * nh + d * nh:2 * nh + (d + 1) * nh])
        cum8 = _dot_hi(incl.astype(F32), g8)
        tot8 = jnp.sum(g8, axis=0, keepdims=True)
        for h in range(nh):
            gc = cum8[:, h:h + 1]
            gm = jnp.broadcast_to(gc, (ck, ck))
            chains.append(dict(
                qkv=qkv_ref, o=o_ref, h=h, si=d * nh + h, incl=incl, strict=strict, gc=gc, bc=b8[:, h:h + 1],
                tot=tot8[:, h:h + 1], decay=jnp.exp(jnp.where(incl, gm - gm.T, NEG_INF)), eg=jnp.exp(gc)))

    def head(c, part):
        off = part * BW + c["h"] * HEAD_DIM
        return c["qkv"][:, off:off + HEAD_DIM]

    for c in chains:
        k = head(c, 1)
        c["kb"] = k * c["bc"]
        c["kf"] = k.astype(BF16)
    for c in chains:
        c["a"] = jnp.where(c["strict"], _dot_nt(c["kb"].astype(BF16), c["kf"]) * c["decay"], 0.0)
    for c in chains:
        c["rhs"] = jnp.concatenate([head(c, 2) * c["bc"], c["kb"] * c["eg"]], axis=1)
    for c in chains:
        c["sol"] = c["rhs"] - _mxu(c["a"], c["rhs"], p)
        c["pw"] = c["a"]
    for _ in range(5):
        for c in chains:
            c["pw"] = _mxu(c["pw"], c["pw"], p)
        for c in chains:
            c["sol"] = c["sol"] + _mxu(c["pw"], c["sol"], p)
    for c in chains:
        c["q"] = head(c, 0) * (HEAD_DIM ** -0.5)
        c["qk"] = jnp.where(c["incl"], _dot_nt(c["q"].astype(BF16), c["kf"]) * c["decay"], 0.0)
    for c in chains:
        c["st"] = s_ref[c["si"]]
        c["stb"] = c["st"].astype(BF16)
        c["vn"] = (c["sol"][:, :HEAD_DIM] - _dot(c["sol"][:, HEAD_DIM:].astype(BF16), c["stb"])).astype(BF16)
    for c in chains:
        sl = slice(c["h"] * HEAD_DIM, (c["h"] + 1) * HEAD_DIM)
        c["o"][:, sl] = (_dot((c["q"] * c["eg"]).astype(BF16), c["stb"])
                         + _dot(c["qk"].astype(BF16), c["vn"]))
    for c in chains:
        kd = head(c, 1) * jnp.exp(c["tot"] - c["gc"])
        s_ref[c["si"]] = c["st"] * jnp.exp(c["tot"]) + _dot_tn(kd.astype(BF16), c["vn"])


def _delta(qkvb, pab, a_log, dt_bias, nl, nc):
    nt = qkvb.shape[0]
    ck = B_CHUNK
    nlc, ncc = nl // ck, nc // ck
    ntc = nlc + ncc
    rf = lambda c: jnp.where(c < ncc, nlc + c, c - ncc)
    rb = lambda c: jnp.where(c < ncc, nlc + ncc - 1 - c, nlc - 1 - (c - ncc))
    vec = pl.BlockSpec((1, 2 * B_HEADS), lambda c: (0, 0))
    out = jax.ShapeDtypeStruct((nt, BW), F32)
    return pl.pallas_call(
        _delta_kernel,
        grid=(ntc,),
        in_specs=[pl.BlockSpec((ck, 3 * BW), lambda c: (rf(c), 0)),
                  pl.BlockSpec((ck, pab.shape[1]), lambda c: (rf(c), 0)),
                  pl.BlockSpec((ck, 3 * BW), lambda c: (rb(c), 0)),
                  pl.BlockSpec((ck, pab.shape[1]), lambda c: (rb(c), 0)),
                  vec, vec],
        out_specs=[pl.BlockSpec((ck, BW), lambda c: (rf(c), 0)), pl.BlockSpec((ck, BW), lambda c: (rb(c), 0))],
        out_shape=[out, out],
        scratch_shapes=[pltpu.VMEM((2 * B_HEADS, HEAD_DIM, HEAD_DIM), F32)],
        compiler_params=_cp(("arbitrary",)),
        name="delta_scan",
    )(qkvb, pab, qkvb, pab, a_log.reshape(1, 2 * B_HEADS), dt_bias.reshape(1, 2 * B_HEADS))


def _dpost_kernel(of_ref, ob_ref, g_ref, ng_ref, o_ref):
    o = of_ref[...] + ob_ref[...]
    gate = g_ref[...]
    ng = ng_ref[...]
    for s in range(o.shape[1] // HEAD_DIM):
        sl = slice(s * HEAD_DIM, (s + 1) * HEAD_DIM)
        os_ = o[:, sl]
        y = os_ * lax.rsqrt(jnp.mean(os_ * os_, axis=-1, keepdims=True) + NORM_EPS) * ng
        o_ref[:, sl] = (y * _silu(gate[:, sl])).astype(o_ref.dtype)


def _dpost(o_f, o_b, pa, norm_g, nc, gcol):
    nt = pa.shape[0]
    tc = 512
    gb = gcol // tc
    return pl.pallas_call(
        _dpost_kernel,
        grid=(nt // nc, BW // tc),
        in_specs=[pl.BlockSpec((nc, tc), lambda i, j: (i, j)),
                  pl.BlockSpec((nc, tc), lambda i, j: (i, j)),
                  pl.BlockSpec((nc, tc), lambda i, j: (i, gb + j)),
                  pl.BlockSpec((1, HEAD_DIM), lambda i, j: (0, 0))],
        out_specs=pl.BlockSpec((nc, tc), lambda i, j: (i, j)),
        out_shape=jax.ShapeDtypeStruct((nt, BW), BF16),
        compiler_params=_cp(("parallel", "parallel")),
        name="delta_post",
    )(o_f, o_b, pa, norm_g.reshape(1, HEAD_DIM))


def _hconv_kernel(a_ref, ap_ref, an_ref, b_ref, bp_ref, bn_ref, c_ref, cp_ref, cn_ref,
                  wa_ref, wb_ref, wc_ref, x0_ref, z_ref, scr, *, nb_lat):
    x0_ref[...] = _conv3(a_ref, ap_ref, an_ref, wa_ref, scr, nb_lat)
    x1 = _conv3(b_ref, bp_ref, bn_ref, wb_ref, scr, nb_lat)
    v = _conv3(c_ref, cp_ref, cn_ref, wc_ref, scr, nb_lat)
    z_ref[...] = v * x1


def _hconv(pc, conv_w, nl, nc):
    nt = pc.shape[0]
    tc = 512
    gpb = BW // tc
    specs = []
    for g in range(3):
        specs += _halo_specs(nc, tc, g * gpb, nt)
    for g in range(3):
        specs.append(pl.BlockSpec((3, tc), lambda i, j, g=g: (0, g * gpb + j)))
    out = pl.BlockSpec((nc, tc), lambda i, j: (i, j))
    return pl.pallas_call(
        functools.partial(_hconv_kernel, nb_lat=nl // nc),
        grid=(nt // nc, gpb),
        in_specs=specs,
        out_specs=[out, out],
        out_shape=[jax.ShapeDtypeStruct((nt, BW), F32)] * 2,
        scratch_shapes=[pltpu.VMEM((nc + 16, tc), F32)],
        compiler_params=_cp(("parallel", "arbitrary")),
        name="hyena_conv",
    )(pc, pc, pc, pc, pc, pc, pc, pc, pc, conv_w, conv_w, conv_w)


def _hfilter_features(n):
    t = np.linspace(0.0, 1.0, n, dtype=np.float32)[:, None]
    bands = (HYENA_EMB - 1) // 2
    wpos = (2.0 * math.pi * np.arange(n, dtype=np.float32)[:, None] / n).astype(np.float32)
    f = np.linspace(1e-4, bands - 1, bands, dtype=np.float32)[None, :]
    z = np.concatenate([t, np.cos(f * wpos), -np.sin(f * wpos)], axis=-1).astype(np.float32)
    pos = np.concatenate([np.arange(n), [0], 2 * n - np.arange(n + 1, 2 * n)])
    zk = np.zeros((2 * n, LANES), np.float32)
    zk[:, :HYENA_EMB] = z[pos]
    max_decay = math.log(1e-2) / 0.3
    min_decay = math.log(1e-2) / 1.5
    absd = np.abs(np.linspace(min_decay, max_decay, BW, dtype=np.float32))[None, :]
    return jnp.asarray(zk), jnp.asarray(absd)


def _hfilter_kernel(z_ref, w1, b1, fr, w2, b2, w3, b3, w4, b4, ad, o_ref, *, n, tr):
    z = z_ref[...]
    frq = fr[...]
    h = jnp.sin(frq * (_mxu(z, w1[...], 3) + b1[...]))
    h = jnp.sin(frq * (_mxu(h, w2[...], 3) + b2[...]))
    h = jnp.sin(frq * (_mxu(h, w3[...], 3) + b3[...]))
    h = _mxu(h, w4[...]) + b4[...]
    win = jnp.exp(-z[:, 0:1] * ad[...])
    row = pl.program_id(0) * tr + lax.broadcasted_iota(I32, (tr, 1), 0)
    o_ref[...] = jnp.where(row == n, 0.0, h * win)


def _hfilter(n, zk, absd, w1, b1, fr, w2, b2, w3, b3, w4, b4):
    tr = min(512, n)
    nfb = n // tr
    w1p = jnp.zeros((LANES, HYENA_FFN), F32).at[:HYENA_EMB].set(w1)
    small = lambda shp: pl.BlockSpec(shp, lambda i: (0, 0))
    row = lambda a: a.reshape(1, -1)
    return pl.pallas_call(
        functools.partial(_hfilter_kernel, n=n, tr=tr),
        grid=(2 * n // tr,),
        in_specs=[pl.BlockSpec((tr, LANES), lambda i: (i, 0)),
                  small((LANES, HYENA_FFN)), small((1, HYENA_FFN)), small((1, HYENA_FFN)),
                  small((HYENA_FFN, HYENA_FFN)), small((1, HYENA_FFN)),
                  small((HYENA_FFN, HYENA_FFN)), small((1, HYENA_FFN)),
                  pl.BlockSpec((HYENA_FFN, BW), lambda i: (0, (i >= nfb).astype(I32))),
                  pl.BlockSpec((1, BW), lambda i: (0, (i >= nfb).astype(I32))),
                  small((1, BW))],
        out_specs=pl.BlockSpec((tr, BW), lambda i: (i, 0)),
        out_shape=jax.ShapeDtypeStruct((2 * n, BW), F32),
        compiler_params=_cp(("parallel",)),
        name="hyena_filter",
    )(zk, w1p, row(b1), row(fr), w2, row(b2), w3, row(b3), w4, row(b4), absd)


def _fft_tables(n):
    big = 2 * n
    h = big // FFT_L
    l2 = FFT_L // 2

    def cs(a, b, period):
        ang = 2.0 * np.pi * np.outer(np.arange(a), np.arange(b)) / period
        return np.cos(ang), np.sin(ang)

    ch, sh = cs(h, h, h)
    cl, sl = cs(FFT_L, FFT_L, FFT_L)
    ct, st = cs(h, FFT_L, big)
    f = lambda a: jnp.asarray(a, F32)
    return dict(
        fhr=f(ch), fhi=f(-sh),
        flr=f(cl[:l2]), fli=f(-sl[:l2]),
        twr=f(ct), twi=f(-st),
        clr=f(cl[:, :l2]), cli=f(sl[:, :l2]),
        chr=f(ch[:h // 2]), chi=f(sh[:h // 2]),
        ctr=f(ct.T), cti=f(st.T),
    )


def _alt_sum(x):
    row = lax.broadcasted_iota(I32, x.shape, 0)
    return jnp.sum(jnp.where((row & 1) == 1, -x, x), axis=0, keepdims=True)


FFT_PITCH = FFT_L + 8
FFT_GROUP = 8


def _lane_halves(x):
    return x[:, :LANES], x[:, LANES:]


def _fft_stage1(x_ref, vr_ref, vi_ref, fhr_ref, fhi_ref, h_in, h):
    fs = jnp.concatenate([fhr_ref[:, 0:h_in], fhi_ref[:, 0:h_in]], axis=0).astype(BF16)
    g = FFT_GROUP

    def body(step, carry):
        m0 = step * g
        cols = [x_ref[pl.ds(m0 + i, h_in, stride=FFT_L), :].astype(BF16) for i in range(g)]
        outs = [_dot(fs, jnp.concatenate(cols[i:i + 2], axis=1)) for i in range(0, g, 2)]
        for i in range(g):
            o = _lane_halves(outs[i // 2])[i % 2]
            vr_ref[pl.ds(m0 + i, h, stride=FFT_PITCH), :] = o[:h]
            vi_ref[pl.ds(m0 + i, h, stride=FFT_PITCH), :] = o[h:]
        return carry

    lax.fori_loop(0, FFT_L // g, body, 0)


def _fft_stage2_blocks(f0, n_blk, vr_ref, vi_ref, flr_ref, fli_ref, twr_ref, twi_ref):
    flr = flr_ref[...]
    fli = fli_ref[...]
    mats, ops = [], []
    for i in range(n_blk):
        twr = twr_ref[pl.ds(f0 + i, 1), :]
        twi = twi_ref[pl.ds(f0 + i, 1), :]
        mats.append(((flr * twr - fli * twi).astype(BF16), (flr * twi + fli * twr).astype(BF16)))
        r0 = pl.multiple_of((f0 + i) * FFT_PITCH, 8)
        ops.append(jnp.concatenate([vr_ref[pl.ds(r0, FFT_L), :], vi_ref[pl.ds(r0, FFT_L), :]],
                                   axis=1).astype(BF16))
    pa = [_dot(m[0], v) for m, v in zip(mats, ops)]
    pb = [_dot(m[1], v) for m, v in zip(mats, ops)]
    out = []
    for a, b in zip(pa, pb):
        ar, ai = _lane_halves(a)
        br, bi = _lane_halves(b)
        out.append((ar - bi, ai + br))
    return out


def _kfft_kernel(k_ref, fhr, fhi, flr, fli, twr, twi, kr_ref, ki_ref, kn_ref, vr_ref, vi_ref, *, h):
    l2 = FFT_L // 2
    g = min(FFT_GROUP, h)
    kn_ref[...] = _alt_sum(k_ref[...])
    _fft_stage1(k_ref, vr_ref, vi_ref, fhr, fhi, h, h)

    def body(step, carry):
        f0 = step * g
        for i, (xr, xi) in enumerate(_fft_stage2_blocks(f0, g, vr_ref, vi_ref, flr, fli, twr, twi)):
            k0 = pl.multiple_of((f0 + i) * l2, l2)
            kr_ref[pl.ds(k0, l2), :] = xr
            ki_ref[pl.ds(k0, l2), :] = xi
        return carry

    lax.fori_loop(0, h // g, body, 0)


def _table_specs(tabs, names):
    return [pl.BlockSpec(tabs[k].shape, lambda j: (0, 0)) for k in names]


def _kfft(kt, tabs, n):
    big = 2 * n
    h = big // FFT_L
    l2 = FFT_L // 2
    names = ["fhr", "fhi", "flr", "fli", "twr", "twi"]
    spec = pl.BlockSpec((h * l2, LANES), lambda j: (0, j))
    return pl.pallas_call(
        functools.partial(_kfft_kernel, h=h),
        grid=(BW // LANES,),
        in_specs=[pl.BlockSpec((big, LANES), lambda j: (0, j))] + _table_specs(tabs, names),
        out_specs=[spec, spec, pl.BlockSpec((1, LANES), lambda j: (0, j))],
        out_shape=[jax.ShapeDtypeStruct((h * l2, BW), F32)] * 2 + [jax.ShapeDtypeStruct((1, BW), F32)],
        scratch_shapes=[pltpu.VMEM((h * FFT_PITCH, LANES), F32)] * 2,
        compiler_params=_cp(("parallel",)),
        name="hyena_filter_fft",
    )(kt, *[tabs[k] for k in names])


def _zfft_kernel(z_ref, x0_ref, kr_ref, ki_ref, kn_ref, d_ref, fhr, fhi, flr, fli, twr, twi,
                 clr, cli, chr_, chi, ctr, cti, o_ref, vr_ref, vi_ref, *, h):
    l2 = FFT_L // 2
    hh = h // 2
    big = h * FFT_L
    pn = _alt_sum(z_ref[...]) * kn_ref[...] * (1.0 / big)
    _fft_stage1(z_ref, vr_ref, vi_ref, fhr, fhi, hh, h)
    cr = clr[...].astype(BF16)
    ci = cli[...].astype(BF16)
    rowi = lax.broadcasted_iota(I32, (l2, 1), 0)
    gm = min(FFT_GROUP // 2, h)

    def mid(step, carry):
        f0 = step * gm
        prods = []
        for i, (xr, xi) in enumerate(_fft_stage2_blocks(f0, gm, vr_ref, vi_ref, flr, fli, twr, twi)):
            k0 = pl.multiple_of((f0 + i) * l2, l2)
            kr = kr_ref[pl.ds(k0, l2), :]
            ki = ki_ref[pl.ds(k0, l2), :]
            half_dc = jnp.where((rowi == 0) & (f0 + i == 0), 0.5, 1.0)
            prods.append(jnp.concatenate([(xr * kr - xi * ki) * half_dc, (xr * ki + xi * kr) * half_dc],
                                         axis=1).astype(BF16))
        za = [_dot(cr, p) for p in prods]
        zb = [_dot(ci, p) for p in prods]
        for i in range(gm):
            ar, ai = _lane_halves(za[i])
            br, bi = _lane_halves(zb[i])
            r0 = pl.multiple_of((f0 + i) * FFT_PITCH, 8)
            vr_ref[pl.ds(r0, FFT_L), :] = ar - bi
            vi_ref[pl.ds(r0, FFT_L), :] = ai + br
        return carry

    lax.fori_loop(0, h // gm, mid, 0)
    ch_r = chr_[...]
    ch_i = chi[...]
    dvec = d_ref[...]
    gl = FFT_GROUP

    def last(step, carry):
        m0 = step * gl
        mats, ops = [], []
        for i in range(gl):
            tr_ = ctr[pl.ds(m0 + i, 1), :]
            ti_ = cti[pl.ds(m0 + i, 1), :]
            mats.append(jnp.concatenate([ch_r * tr_ - ch_i * ti_, -(ch_r * ti_ + ch_i * tr_)], axis=1).astype(BF16))
            ops.append(jnp.concatenate([vr_ref[pl.ds(m0 + i, h, stride=FFT_PITCH), :],
                                        vi_ref[pl.ds(m0 + i, h, stride=FFT_PITCH), :]], axis=0).astype(BF16))
        ys = [_dot(m, v) for m, v in zip(mats, ops)]
        for i in range(gl):
            sgn = jnp.where(((m0 + i) & 1) == 1, -1.0, 1.0)
            y = ys[i] * (2.0 / big) + pn * sgn
            zt = z_ref[pl.ds(m0 + i, hh, stride=FFT_L), :]
            x0 = x0_ref[pl.ds(m0 + i, hh, stride=FFT_L), :]
            o_ref[pl.ds(m0 + i, hh, stride=FFT_L), :] = (y + dvec * zt) * x0
        return carry

    lax.fori_loop(0, FFT_L // gl, last, 0)


def _zfft(z, x0c, kr, ki, kn, d_bias, tabs, nl):
    big = 2 * nl
    h = big // FFT_L
    l2 = FFT_L // 2
    names = ["fhr", "fhi", "flr", "fli", "twr", "twi", "clr", "cli", "chr", "chi", "ctr", "cti"]
    one = pl.Buffered(1)
    col = lambda rows: pl.BlockSpec((rows, LANES), lambda j: (0, j), pipeline_mode=one)
    vec = pl.BlockSpec((1, LANES), lambda j: (0, j))
    return pl.pallas_call(
        functools.partial(_zfft_kernel, h=h),
        grid=(BW // LANES,),
        in_specs=[col(nl), col(nl), col(h * l2), col(h * l2), vec, vec] + _table_specs(tabs, names),
        out_specs=pl.BlockSpec((nl, LANES), lambda j: (0, j)),
        out_shape=jax.ShapeDtypeStruct((nl, BW), F32),
        scratch_shapes=[pltpu.VMEM((h * FFT_PITCH, LANES), F32)] * 2,
        compiler_params=_cp(("parallel",)),
        name="hyena_fftconv",
    )(z, x0c, kr, ki, kn, d_bias.reshape(1, BW), *[tabs[k] for k in names])


def _hsmall_kernel(z_ref, x0_ref, k_ref, d_ref, fr_ref, fi_ref, o_ref, *, nc):
    big = 2 * nc
    fr = fr_ref[...].astype(BF16)
    fi = fi_ref[...].astype(BF16)
    z = z_ref[...]
    zb = z.astype(BF16)
    kb = k_ref[...].astype(BF16)
    xr = _dot(fr[:, :nc], zb)
    xi = _dot(fi[:, :nc], zb)
    kr = _dot(fr, kb)
    ki = _dot(fi, kb)
    pr = (xr * kr - xi * ki).astype(BF16)
    pi = (xr * ki + xi * kr).astype(BF16)
    y = (_dot(fr[:nc, :], pr) + _dot(fi[:nc, :], pi)) * (1.0 / big)
    o_ref[...] = (y + d_ref[...] * z) * x0_ref[...]


def _hsmall(z, x0c, kt_c, d_bias, nl, nc):
    big = 2 * nc
    ang = 2.0 * np.pi * np.outer(np.arange(big), np.arange(big)) / big
    fr = jnp.asarray(np.cos(ang), F32)
    fi = jnp.asarray(-np.sin(ang), F32)
    tc = 256
    rb = nl // nc
    return pl.pallas_call(
        functools.partial(_hsmall_kernel, nc=nc),
        grid=(BW // tc,),
        in_specs=[pl.BlockSpec((nc, tc), lambda j: (rb, j)),
                  pl.BlockSpec((nc, tc), lambda j: (rb, j)),
                  pl.BlockSpec((big, tc), lambda j: (0, j)),
                  pl.BlockSpec((1, tc), lambda j: (0, j)),
                  pl.BlockSpec((big, big), lambda j: (0, 0)),
                  pl.BlockSpec((big, big), lambda j: (0, 0))],
        out_specs=pl.BlockSpec((nc, tc), lambda j: (0, j)),
        out_shape=jax.ShapeDtypeStruct((nc, BW), F32),
        compiler_params=_cp(("parallel",)),
        name="hyena_ctx",
    )(z, x0c, kt_c, d_bias.reshape(1, BW), fr, fi)


def _merge_kernel(ya, yb, ycl, ycc, yd, g0, g1, g2, g3, wb_ref, o_ref):
    is_ctx = pl.program_id(1) == pl.num_programs(1) - 1
    yc = jnp.where(is_ctx, ycc[...], ycl[...])
    acc = None
    for b, (y, g) in enumerate(((ya[...], g0), (yb[...], g1), (yc, g2), (yd[...], g3))):
        p = _dot(y.astype(BF16), wb_ref[b].astype(BF16))
        t = jax.nn.sigmoid(g[...]) * p
        acc = t if acc is None else acc + t
    o_ref[...] = acc.astype(o_ref.dtype)


def _merge(ya, yb, yc_lat, yc_ctx, yd, pc, w_branch, l, gcol, nc):
    nt = ya.shape[0]
    tn = 512
    d = D_MODEL
    nb_lat = yc_lat.shape[0] // nc
    ysp = pl.BlockSpec((nc, BW), lambda j, i: (i, 0))
    gsp = [pl.BlockSpec((nc, tn), lambda j, i, b=b: (i, (gcol + b * d) // tn + j)) for b in range(4)]
    return pl.pallas_call(
        _merge_kernel,
        grid=(d // tn, nt // nc),
        in_specs=[ysp, ysp,
                  pl.BlockSpec((nc, BW), lambda j, i: (jnp.minimum(i, nb_lat - 1), 0)),
                  pl.BlockSpec((nc, BW), lambda j, i: (0, 0)),
                  ysp] + gsp + [pl.BlockSpec((None, 4, BW, tn), lambda j, i: (l, 0, 0, j))],
        out_specs=pl.BlockSpec((nc, tn), lambda j, i: (i, j)),
        out_shape=jax.ShapeDtypeStruct((nt, d), BF16),
        compiler_params=_cp(("parallel", "arbitrary")),
        name="merge",
    )(ya, yb, yc_lat, yc_ctx, yd, pc, pc, pc, pc, w_branch)


def _post_norm(x, delta, gate, lng, lnb):
    y = ALPHA * x + gate * delta
    mu = jnp.mean(y, axis=-1, keepdims=True)
    yc = y - mu
    var = jnp.mean(yc * yc, axis=-1, keepdims=True)
    return yc * lax.rsqrt(var + LN_EPS) * lng + lnb


def _ln1_kernel(x_ref, dl_ref, g_ref, lng_ref, lnb_ref, sh_ref, sc_ref, wr_ref, xo_ref, uo_ref, lg_ref):
    is_ctx = pl.program_id(0) == pl.num_programs(0) - 1
    xn = _post_norm(x_ref[...], dl_ref[...], _pick_mod(g_ref[...], is_ctx), lng_ref[...], lnb_ref[...])
    xo_ref[...] = xn
    u = xn * (1.0 + _pick_mod(sc_ref[...], is_ctx)) + _pick_mod(sh_ref[...], is_ctx)
    uo_ref[...] = u
    lg_ref[...] = _dot_nt_hi(wr_ref[...], u)


def _ln1(x, mix, mod_l, lng, lnb, w_router, nc):
    nt, d = x.shape
    row = pl.BlockSpec((nc, d), lambda i: (i, 0))
    modc = lambda c: pl.BlockSpec((8, d), lambda i: (0, c))
    vec = pl.BlockSpec((1, d), lambda i: (0, 0))
    return pl.pallas_call(
        _ln1_kernel,
        grid=(nt // nc,),
        in_specs=[row, row, modc(2), vec, vec, modc(3), modc(4),
                  pl.BlockSpec((N_EXPERTS, d), lambda i: (0, 0))],
        out_specs=[row, row, pl.BlockSpec((N_EXPERTS, nc), lambda i: (0, i))],
        out_shape=[jax.ShapeDtypeStruct((nt, d), F32), jax.ShapeDtypeStruct((nt, d), F32),
                   jax.ShapeDtypeStruct((N_EXPERTS, nt), F32)],
        compiler_params=_cp(("parallel",)),
        name="ln1_router",
    )(x, mix, mod_l, lng.reshape(1, d), lnb.reshape(1, d), mod_l, mod_l, w_router.T)


def _cumsum_lanes(x):
    n = x.shape[1]
    lane = lax.broadcasted_iota(I32, x.shape, 1)
    s = 1
    while s < n:
        x = x + jnp.where(lane >= s, pltpu.roll(x, s, 1), 0.0)
        s *= 2
    return x


def _select_kernel(lg_ref, idx_ref, w_ref, posx_ref, posm_ref, aff_scr, pos_scr, *, n, cap, sc):
    lg = lg_ref[...]
    e = jnp.exp(lg - jnp.max(lg, axis=0, keepdims=True))
    aff = e / jnp.sum(e, axis=0, keepdims=True)
    bits = pltpu.bitcast(aff, I32)
    tau = jnp.zeros((N_EXPERTS, 1), I32)
    for bit in range(30, -1, -1):
        cand = tau | (1 << bit)
        cnt = jnp.sum((bits >= cand).astype(F32), axis=1, keepdims=True)
        tau = jnp.where(cnt >= cap, cand, tau)
    gt = bits > tau
    eq = (bits == tau).astype(F32)
    need = cap - jnp.sum(gt.astype(F32), axis=1, keepdims=True)
    eq_rank = _cumsum_lanes(eq) - eq
    sel = jnp.where(gt, 1.0, jnp.where(eq_rank < need, eq, 0.0))
    posx = _cumsum_lanes(sel) - sel
    posx_ref[...] = posx.astype(I32)
    posm = jnp.where(sel > 0.0, posx, -1.0)
    posm_ref[...] = posm.astype(I32)
    aff_scr[...] = aff
    pos_scr[...] = posm
    nch = cap // sc
    tok = lax.broadcasted_iota(I32, (sc, n), 1).astype(F32)
    slot0 = lax.broadcasted_iota(I32, (sc, 1), 0).astype(F32)
    lane = lax.broadcasted_iota(I32, (sc, LANES), 1)

    def per_expert(ex, carry):
        it, wt = carry
        prow = pos_scr[pl.ds(ex, 1), :]
        arow = aff_scr[pl.ds(ex, 1), :]
        for ch in range(nch):
            hit = prow == (slot0 + float(ch * sc))
            icol = jnp.sum(jnp.where(hit, tok, 0.0), axis=1, keepdims=True)
            wcol = jnp.sum(jnp.where(hit, arow, 0.0), axis=1, keepdims=True)
            here = lane == ex * nch + ch
            it = jnp.where(here, icol, it)
            wt = jnp.where(here, wcol, wt)
        return it, wt

    zero = jnp.zeros((sc, LANES), F32)
    it, wt = lax.fori_loop(0, N_EXPERTS, per_expert, (zero, zero))
    idx_ref[...] = it.astype(I32)
    w_ref[...] = wt


def _select(lg_t, col0, n, cap):
    sc = min(LANES, cap)
    nch = cap // sc
    assert nch * N_EXPERTS <= LANES and col0 % n == 0
    idx_t, w_t, posx, posm = pl.pallas_call(
        functools.partial(_select_kernel, n=n, cap=cap, sc=sc),
        grid=(1,),
        in_specs=[pl.BlockSpec((N_EXPERTS, n), lambda i: (0, col0 // n))],
        out_specs=[pl.BlockSpec((sc, LANES), lambda i: (0, 0)), pl.BlockSpec((sc, LANES), lambda i: (0, 0)),
                   pl.BlockSpec((N_EXPERTS, n), lambda i: (0, 0)), pl.BlockSpec((N_EXPERTS, n), lambda i: (0, 0))],
        out_shape=[jax.ShapeDtypeStruct((sc, LANES), I32), jax.ShapeDtypeStruct((sc, LANES), F32),
                   jax.ShapeDtypeStruct((N_EXPERTS, n), I32), jax.ShapeDtypeStruct((N_EXPERTS, n), I32)],
        scratch_shapes=[pltpu.VMEM((N_EXPERTS, n), F32)] * 2,
        compiler_params=_cp(("arbitrary",)),
        name="moe_select",
    )(lg_t)
    unt = lambda a: a[:, :N_EXPERTS * nch].T.reshape(N_EXPERTS, cap)
    return unt(idx_t), unt(w_t), posx, posm


def _ffn_kernel(idx_ref, u_hbm, ws_ref, w1_ref, w3_ref, w2_ref, o_ref, xg, xb, acc, sem, *, capt, capp):
    ex = pl.program_id(0)
    f = pl.program_id(1)

    def row_copy(r, src_row):
        return pltpu.make_async_copy(u_hbm.at[pl.ds(src_row, 1)], xg.at[pl.ds(r, 1)], sem)

    def gather(which):
        def issue(r, carry):
            row_copy(r, idx_ref[which, r]).start()
            return carry

        lax.fori_loop(0, capt, issue, 0, unroll=8)

    @pl.when(f == 0)
    def _():
        @pl.when(ex == 0)
        def _():
            gather(ex)

        def drain(r, carry):
            row_copy(r, 0).wait()
            return carry

        lax.fori_loop(0, capt, drain, 0, unroll=8)
        xb[...] = xg[...].astype(BF16)

        @pl.when(ex + 1 < pl.num_programs(0))
        def _():
            gather(ex + 1)

        acc[...] = jnp.zeros_like(acc)

    x = xb[...]
    hdn = _silu(_dot(x, w1_ref[...].astype(BF16))) * _dot(x, w3_ref[...].astype(BF16))
    acc[...] += _dot(hdn.astype(BF16), w2_ref[...].astype(BF16))

    @pl.when(f == pl.num_programs(1) - 1)
    def _():
        o_ref[0:capt, :] = (acc[...] * ws_ref[...]).astype(o_ref.dtype)
        o_ref[capt:capp, :] = jnp.zeros((capp - capt, o_ref.shape[1]), o_ref.dtype)


def _ffn(idx, u2, wsel, w_e1, w_e3, w_e2, l, capp):
    capt = idx.shape[1]
    d = D_MODEL
    tf = 256
    grid_spec = pltpu.PrefetchScalarGridSpec(
        num_scalar_prefetch=1,
        grid=(N_EXPERTS, EXPERT_FF // tf),
        in_specs=[pl.BlockSpec(memory_space=pl.ANY),
                  pl.BlockSpec((None, capt, 1), lambda e, f, idx: (e, 0, 0)),
                  pl.BlockSpec((None, None, d, tf), lambda e, f, idx: (l, e, 0, f)),
                  pl.BlockSpec((None, None, d, tf), lambda e, f, idx: (l, e, 0, f)),
                  pl.BlockSpec((None, None, tf, d), lambda e, f, idx: (l, e, f, 0))],
        out_specs=pl.BlockSpec((None, capp, d), lambda e, f, idx: (e, 0, 0)),
        scratch_shapes=[pltpu.VMEM((capt, d), F32), pltpu.VMEM((capt, d), BF16), pltpu.VMEM((capt, d), F32),
                        pltpu.SemaphoreType.DMA(())],
    )
    return pl.pallas_call(
        functools.partial(_ffn_kernel, capt=capt, capp=capp),
        grid_spec=grid_spec,
        out_shape=jax.ShapeDtypeStruct((N_EXPERTS, capp, d), BF16),
        compiler_params=_cp(("arbitrary", "arbitrary")),
        name="moe_ffn",
    )(idx, u2, wsel[..., None], w_e1, w_e3, w_e2)


COMB_CHUNK = 128
COMB_ALIGN = 16


def _combine_kernel(st_ref, nch_ref, posm_ref, y_hbm, x_ref, g_ref, lng_ref, lnb_ref, sh_ref, sc_ref,
                    xo_ref, uo_ref, ybuf, yext, acc_ref, sem, sem_x):
    b = pl.program_id(0)
    nb = pl.num_programs(0)
    is_ctx = b == nb - 1
    nc = x_ref.shape[0]
    ch = COMB_CHUNK

    def base_of(ex, blk):
        s0 = st_ref[ex, blk]
        return pl.multiple_of(s0 - (s0 & (COMB_ALIGN - 1)), COMB_ALIGN)

    def first_copy(ex, blk, slot):
        return pltpu.make_async_copy(y_hbm.at[ex, pl.ds(base_of(ex, blk), ch)],
                                     ybuf.at[slot, pl.ds(ex * ch, ch)], sem.at[slot])

    def issue(blk, slot):
        for ex in range(N_EXPERTS):
            first_copy(ex, blk, slot).start()

    slot = b % 2

    @pl.when(b == 0)
    def _():
        issue(b, slot)

    @pl.when(b + 1 < nb)
    def _():
        issue(b + 1, 1 - slot)

    for ex in range(N_EXPERTS):
        first_copy(ex, b, slot).wait()
    posm = posm_ref[...]
    lane = lax.broadcasted_iota(I32, (nc, ch), 1)
    rels = [posm[:, ex:ex + 1] - base_of(ex, b) for ex in range(N_EXPERTS)]
    onehot = jnp.concatenate([(r == lane).astype(BF16) for r in rels], axis=1)
    acc_ref[...] = _dot(onehot, ybuf[slot])
    for ex in range(N_EXPERTS):
        def extra(j, carry, ex=ex):
            cp = pltpu.make_async_copy(y_hbm.at[ex, pl.ds(base_of(ex, b) + j * ch, ch)], yext, sem_x)
            cp.start()
            cp.wait()
            acc_ref[...] += _dot(((rels[ex] - j * ch) == lane).astype(BF16), yext[...])
            return carry

        lax.fori_loop(1, nch_ref[ex, b], extra, 0)
    xn = _post_norm(x_ref[...], acc_ref[...], _pick_mod(g_ref[...], is_ctx), lng_ref[...], lnb_ref[...])
    xo_ref[...] = xn
    u = xn * (1.0 + _pick_mod(sc_ref[...], is_ctx)) + _pick_mod(sh_ref[...], is_ctx)
    uo_ref[...] = u.astype(uo_ref.dtype)


def _combine(starts, counts, posm, y, x1, mod_l, lng, lnb, mod_next, nc):
    nt, d = x1.shape
    nch = (((starts & (COMB_ALIGN - 1)) + counts + COMB_CHUNK - 1) // COMB_CHUNK).astype(I32)
    row = lambda: pl.BlockSpec((nc, d), lambda i, st, nq: (i, 0))
    modc = lambda c: pl.BlockSpec((8, d), lambda i, st, nq: (0, c))
    vec = pl.BlockSpec((1, d), lambda i, st, nq: (0, 0))
    grid_spec = pltpu.PrefetchScalarGridSpec(
        num_scalar_prefetch=2,
        grid=(nt // nc,),
        in_specs=[pl.BlockSpec((nc, N_EXPERTS), lambda i, st, nq: (i, 0)),
                  pl.BlockSpec(memory_space=pl.ANY),
                  row(), modc(5), vec, vec, modc(0), modc(1)],
        out_specs=[row(), row()],
        scratch_shapes=[pltpu.VMEM((2, N_EXPERTS * COMB_CHUNK, d), BF16), pltpu.VMEM((COMB_CHUNK, d), BF16),
                        pltpu.VMEM((nc, d), F32), pltpu.SemaphoreType.DMA((2,)), pltpu.SemaphoreType.DMA(())],
    )
    return pl.pallas_call(
        _combine_kernel,
        grid_spec=grid_spec,
        out_shape=[jax.ShapeDtypeStruct((nt, d), F32), jax.ShapeDtypeStruct((nt, d), BF16)],
        compiler_params=_cp(("arbitrary",)),
        name="moe_combine_ln2",
    )(starts, nch, posm.T, y, x1, mod_l, lng.reshape(1, d), lnb.reshape(1, d), mod_next, mod_next)


def _row_tile(nt, nc):
    return nt // 4 if (nt % 4 == 0 and (nt // 4) % 16 == 0 and nt // 4 >= nc) else nc


def kernel(x, c, ctx, c_ctx, w_mod, b_mod, w_in, attn_sink, delta_conv, delta_a_log, delta_dt_bias, delta_norm_g, hyena_conv, hyena_w1, hyena_b1, hyena_freq, hyena_w2, hyena_b2, hyena_w3, hyena_b3, hyena_w4, hyena_b4, hyena_d, na_rel_bias, w_branch, w_o, ln1_g, ln1_b, ln2_g, ln2_b, w_router, w_e1, w_e3, w_e2):
    nl, nc = x.shape[1], ctx.shape[1]
    nt = nl + nc
    depth = w_in.shape[0]
    tm = _row_tile(nt, nc)
    cap_l = EC_CAPACITY * nl // N_EXPERTS
    cap_c = EC_CAPACITY * nc // N_EXPERTS
    capp = cap_l + cap_c + COMB_CHUNK

    xc = jnp.concatenate([x[0], ctx[0]], axis=0)
    mod = _modvec(c, c_ctx, w_mod, b_mod)
    cos_t, sin_t = _rope_tables(nl, nc)
    tabs = _fft_tables(nl)
    zk_l, absd = _hfilter_features(nl)
    zk_c, _ = _hfilter_features(nc)
    u = _modulate(xc, mod[0], nc)

    for l in range(depth):
        pa = _matmul(u, w_in, l, 0, SEG1, tm, 512, name="in_proj_a")
        pab = _matmul(u, w_in, l, SEG1, LANES, tm, LANES, name="in_proj_ab")
        pc = _matmul_shifted(u, w_in, l, SEG1, SEG_AB, SEG3, tm, 512, name="in_proj_c")
        ya = _attn_a(_rope(pa, cos_t, sin_t, nc), attn_sink[l], nl, nc)
        qkvb = _dprep(pa, delta_conv[l], nl, nc, A_HEADS * HEAD_DIM + 2 * A_KV * HEAD_DIM)
        o_f, o_b = _delta(qkvb, pab, delta_a_log[l], delta_dt_bias[l], nl, nc)
        yb = _dpost(o_f, o_b, pa, delta_norm_g[l], nc, SEG1 - BW)
        filt = (hyena_w1[l], hyena_b1[l], hyena_freq[l], hyena_w2[l], hyena_b2[l], hyena_w3[l], hyena_b3[l],
                hyena_w4[l], hyena_b4[l])
        x0c, z = _hconv(pc, hyena_conv[l], nl, nc)
        kr, ki, kn = _kfft(_hfilter(nl, zk_l, absd, *filt), tabs, nl)
        yc_lat = _zfft(z, x0c, kr, ki, kn, hyena_d[l], tabs, nl)
        yc_ctx = _hsmall(z, x0c, _hfilter(nc, zk_c, absd, *filt), hyena_d[l], nl, nc)
        yd = _na(pc, _na_bias_table(na_rel_bias[l]), nl, nc, 3 * BW)
        mrg = _merge(ya, yb, yc_lat, yc_ctx, yd, pc, w_branch, l, 6 * BW, nc)
        mix = _matmul(mrg, w_o, l, 0, D_MODEL, tm, 512, name="out_proj")
        x1, u2, lg_t = _ln1(xc, mix, mod[l], ln1_g[l], ln1_b[l], w_router[l], nc)
        idx_l, ws_l, posx_l, posm_l = _select(lg_t, 0, nl, cap_l)
        idx_c, ws_c, posx_c, posm_c = _select(lg_t, nl, nc, cap_c)
        idx = jnp.concatenate([idx_l, idx_c + nl], axis=1)
        wsel = jnp.concatenate([ws_l, ws_c], axis=1)
        y = _ffn(idx, u2, wsel, w_e1, w_e3, w_e2, l, capp)
        posm = jnp.concatenate([posm_l, jnp.where(posm_c >= 0, posm_c + cap_l, -1)], axis=1)
        st_l = posx_l[:, ::nc]
        cnt_l = jnp.concatenate([st_l[:, 1:], jnp.full((N_EXPERTS, 1), cap_l, I32)], axis=1) - st_l
        starts = jnp.concatenate([st_l, posx_c[:, :1] + cap_l], axis=1)
        counts = jnp.concatenate([cnt_l, jnp.full((N_EXPERTS, 1), cap_c, I32)], axis=1)
        xc, u = _combine(starts, counts, posm, y, x1, mod[l], ln2_g[l], ln2_b[l], mod[min(l + 1, depth - 1)], nc)
    return xc[:nl][None]
```

```python
import functools
import math

import numpy as np
import jax
import jax.numpy as jnp
from jax import lax
from jax.experimental import pallas as pl
from jax.experimental.pallas import tpu as pltpu

F32 = jnp.float32
BF16 = jnp.bfloat16
I32 = jnp.int32
HI = lax.Precision.HIGHEST

D_MODEL = 2048
GRID_W = 64
HEAD_DIM = 128
LANES = 128
BW = D_MODEL // 2
A_HEADS, A_KV = 8, 2
A_WINDOW = 128
A_BLK = 128
B_HEADS = 8
B_CHUNK = 64
D_HEADS = 8
NA_KH, NA_KW = 8, 16
N_EXPERTS = 16
EXPERT_FF = D_MODEL // 2
EC_CAPACITY = 2
HYENA_EMB, HYENA_FFN = 33, 64
ROPE_BASE = 10000.0
LN_EPS = 1e-6
NORM_EPS = 1e-6
NEG_INF = -1e30
DEPTH = 4
ALPHA = (2.0 * DEPTH) ** 0.25

SEG1 = 5632
SEG_AB = 32
SEG_CD = 6144
N_BRANCH = 4
FFT_L = 128

VMEM_LIMIT = 56 * 1024 * 1024


def _cp(sem=None, vmem=VMEM_LIMIT):
    kw = dict(vmem_limit_bytes=vmem)
    if sem is not None:
        kw["dimension_semantics"] = sem
    return pltpu.CompilerParams(**kw)


def _dot(a, b):
    return jnp.dot(a, b, preferred_element_type=F32)


def _dot_hi(a, b):
    return jnp.dot(a, b, preferred_element_type=F32, precision=HI)


def _dot_nt(a, b):
    return lax.dot_general(a, b, (((1,), (1,)), ((), ())), preferred_element_type=F32)


def _dot_nt_hi(a, b):
    return lax.dot_general(a, b, (((1,), (1,)), ((), ())), preferred_element_type=F32, precision=HI)


def _dot_tn(a, b):
    return lax.dot_general(a, b, (((0,), (0,)), ((), ())), preferred_element_type=F32)


def _silu(x):
    return x * jax.nn.sigmoid(x)


def _modvec_kernel(ct_ref, w_ref, b_ref, o_ref):
    s = _silu(ct_ref[...])
    w = w_ref[...]
    r0 = jnp.sum(w * s[:, 0:1], axis=0, keepdims=True) + b_ref[...]
    r1 = jnp.sum(w * s[:, 1:2], axis=0, keepdims=True) + b_ref[...]
    o_ref[...] = jnp.concatenate([r0, r1, jnp.zeros((6, w.shape[1]), F32)], axis=0)


def _modvec(c, c_ctx, w_mod, b_mod):
    depth, d, n6 = w_mod.shape
    tn = 512
    ct = jnp.zeros((d, LANES), F32).at[:, 0].set(c[0]).at[:, 1].set(c_ctx)
    return pl.pallas_call(
        _modvec_kernel,
        grid=(depth, n6 // tn),
        in_specs=[pl.BlockSpec((d, LANES), lambda l, j: (0, 0)),
                  pl.BlockSpec((None, d, tn), lambda l, j: (l, 0, j)),
                  pl.BlockSpec((None, 1, tn), lambda l, j: (l, 0, j))],
        out_specs=pl.BlockSpec((None, 8, tn), lambda l, j: (l, 0, j)),
        out_shape=jax.ShapeDtypeStruct((depth, 8, n6), F32),
        compiler_params=_cp(("parallel", "parallel")),
        name="modvec",
    )(ct, w_mod, b_mod.reshape(depth, 1, n6))


def _pick_mod(m, is_ctx):
    return jnp.where(is_ctx, m[1:2, :], m[0:1, :])


def _modulate_kernel(x_ref, sh_ref, sc_ref, o_ref):
    is_ctx = pl.program_id(0) == pl.num_programs(0) - 1
    sh = _pick_mod(sh_ref[...], is_ctx)
    sc = _pick_mod(sc_ref[...], is_ctx)
    o_ref[...] = (x_ref[...] * (1.0 + sc) + sh).astype(o_ref.dtype)


def _modulate(x, mod_l, nc):
    nt, d = x.shape
    return pl.pallas_call(
        _modulate_kernel,
        grid=(nt // nc,),
        in_specs=[pl.BlockSpec((nc, d), lambda i: (i, 0)),
                  pl.BlockSpec((8, d), lambda i: (0, 0)),
                  pl.BlockSpec((8, d), lambda i: (0, 1))],
        out_specs=pl.BlockSpec((nc, d), lambda i: (i, 0)),
        out_shape=jax.ShapeDtypeStruct((nt, d), BF16),
        compiler_params=_cp(("parallel",)),
        name="modulate",
    )(x, mod_l, mod_l)


def _mm_kernel(a_ref, w_ref, o_ref):
    o_ref[...] = _dot(a_ref[...], w_ref[...].astype(BF16)).astype(o_ref.dtype)


def _matmul(a, w, l, col0, n, tm, tn, out_dtype=F32, name="mm"):
    m, k = a.shape
    assert m % tm == 0 and n % tn == 0 and col0 % tn == 0
    cb = col0 // tn
    return pl.pallas_call(
        _mm_kernel,
        grid=(m // tm, n // tn),
        in_specs=[pl.BlockSpec((tm, k), lambda i, j: (i, 0)),
                  pl.BlockSpec((None, k, tn), lambda i, j: (l, 0, cb + j))],
        out_specs=pl.BlockSpec((tm, tn), lambda i, j: (i, j)),
        out_shape=jax.ShapeDtypeStruct((m, n), out_dtype),
        compiler_params=_cp(("parallel", "parallel")),
        name=name,
    )(a, w)


def _mm_t_kernel(a_ref, w_ref, wt_ref, o_ref, *, shift):
    w = w_ref[...]
    if shift:
        w = jnp.concatenate([w[shift:], wt_ref[...]], axis=0)
    o_ref[...] = _dot_nt(a_ref[...], w.astype(BF16)).astype(o_ref.dtype)


def _matmul_t(a, wt, l, row0, shift, n, tm, tn, out_dtype=F32, name="mm_t"):
    m, k = a.shape
    assert m % tm == 0 and n % tn == 0 and row0 % tn == 0 and shift % 8 == 0 and tn % max(shift, 8) == 0
    rb = row0 // tn
    ts = max(shift, 8)
    return pl.pallas_call(
        functools.partial(_mm_t_kernel, shift=shift),
        grid=(m // tm, n // tn),
        in_specs=[pl.BlockSpec((tm, k), lambda i, j: (i, 0)),
                  pl.BlockSpec((None, tn, k), lambda i, j: (l, rb + j, 0)),
                  pl.BlockSpec((None, ts, k), lambda i, j: (l, (rb + j + 1) * (tn // ts), 0))],
        out_specs=pl.BlockSpec((tm, tn), lambda i, j: (i, j)),
        out_shape=jax.ShapeDtypeStruct((m, n), out_dtype),
        compiler_params=_cp(("parallel", "parallel")),
        name=name,
    )(a, wt, wt)


def _rope_tables(nl, nc):
    half = HEAD_DIM // 2
    nf = half // 2
    inv_freq = ROPE_BASE ** (-jnp.arange(nf, dtype=F32) / nf)
    tpos = jnp.arange(nl)
    ang_r = (tpos // GRID_W).astype(F32)[:, None] * inv_freq[None, :]
    ang_c = (tpos % GRID_W).astype(F32)[:, None] * inv_freq[None, :]
    cr, sr, cc, sc = jnp.cos(ang_r), jnp.sin(ang_r), jnp.cos(ang_c), jnp.sin(ang_c)
    cos_t = jnp.concatenate([cr, cr, cc, cc], axis=1)
    sin_t = jnp.concatenate([-sr, sr, -sc, sc], axis=1)
    cos_t = jnp.concatenate([cos_t, jnp.ones((nc, HEAD_DIM), F32)], axis=0)
    sin_t = jnp.concatenate([sin_t, jnp.zeros((nc, HEAD_DIM), F32)], axis=0)
    return cos_t, sin_t


def _rope_kernel(x_ref, c_ref, s_ref, o_ref, *, n_rot, n_sl):
    cos_t = c_ref[...]
    sin_t = s_ref[...]
    lane = lax.broadcasted_iota(I32, cos_t.shape, 1)
    quarter = HEAD_DIM // 4
    first = (lane % (2 * quarter)) < quarter
    for j in range(n_sl):
        sl = slice(j * HEAD_DIM, (j + 1) * HEAD_DIM)
        x = x_ref[:, sl]
        if j < n_rot:
            swapped = jnp.where(first, pltpu.roll(x, HEAD_DIM - quarter, 1), pltpu.roll(x, quarter, 1))
            x = x * cos_t + swapped * sin_t
        o_ref[:, sl] = x.astype(o_ref.dtype)


def _rope(pa, cos_t, sin_t, tr):
    nt = pa.shape[0]
    n_rot = A_HEADS + A_KV
    n_sl = A_HEADS + 2 * A_KV
    width = n_sl * HEAD_DIM
    return pl.pallas_call(
        functools.partial(_rope_kernel, n_rot=n_rot, n_sl=n_sl),
        grid=(nt // tr,),
        in_specs=[pl.BlockSpec((tr, width), lambda i: (i, 0)),
                  pl.BlockSpec((tr, HEAD_DIM), lambda i: (i, 0)),
                  pl.BlockSpec((tr, HEAD_DIM), lambda i: (i, 0))],
        out_specs=pl.BlockSpec((tr, width), lambda i: (i, 0)),
        out_shape=jax.ShapeDtypeStruct((nt, width), BF16),
        compiler_params=_cp(("parallel",)),
        name="rope",
    )(pa, cos_t, sin_t)


def _attn_a_kernel(sink_ref, q_ref, k_ref, v_ref, o_ref, *, nl, nc):
    h = pl.program_id(0)
    qb = pl.program_id(1)
    grp = A_HEADS // A_KV
    scale = HEAD_DIM ** -0.5
    q = q_ref[...]
    qs = jnp.concatenate([q[:, g * HEAD_DIM:(g + 1) * HEAD_DIM] for g in range(grp)], axis=0)
    rows = grp * A_BLK
    wlen = 3 * A_BLK
    is_lat = qb < nl // A_BLK
    ws = pl.multiple_of(jnp.clip((qb - 1) * A_BLK, 0, nl - wlen), A_BLK)
    kc = k_ref[nl:nl + nc, :]
    vc = v_ref[nl:nl + nc, :]
    kw = k_ref[pl.ds(ws, wlen), :]
    vw = v_ref[pl.ds(ws, wlen), :]
    s_c = _dot_nt(qs, kc) * scale
    s_l = _dot_nt(qs, kw) * scale
    ri = lax.broadcasted_iota(I32, (rows, wlen), 0)
    ci = lax.broadcasted_iota(I32, (rows, wlen), 1)
    qabs = qb * A_BLK + (ri % A_BLK)
    kabs = ws + ci
    mask = is_lat & (jnp.abs(kabs - qabs) <= A_WINDOW)
    s_l = jnp.where(mask, s_l, NEG_INF)
    rcol = lax.broadcasted_iota(I32, (rows, 1), 0) // A_BLK
    sink = jnp.zeros((rows, 1), F32)
    for g in range(grp):
        sink = jnp.where(rcol == g, sink_ref[h * grp + g], sink)
    m = jnp.maximum(jnp.maximum(jnp.max(s_c, axis=-1, keepdims=True),
                                jnp.max(s_l, axis=-1, keepdims=True)), sink)
    p_c = jnp.exp(s_c - m)
    p_l = jnp.exp(s_l - m)
    den = (jnp.sum(p_c, axis=-1, keepdims=True) + jnp.sum(p_l, axis=-1, keepdims=True)
           + jnp.exp(sink - m))
    y = (_dot(p_c.astype(BF16), vc) + _dot(p_l.astype(BF16), vw)) / den
    for g in range(grp):
        o_ref[:, g * HEAD_DIM:(g + 1) * HEAD_DIM] = y[g * A_BLK:(g + 1) * A_BLK].astype(o_ref.dtype)


def _attn_a(qkv, sink, nl, nc):
    nt = qkv.shape[0]
    grp = A_HEADS // A_KV
    qw = grp * HEAD_DIM
    kcol = A_HEADS
    vcol = A_HEADS + A_KV
    return pl.pallas_call(
        functools.partial(_attn_a_kernel, nl=nl, nc=nc),
        grid=(A_KV, nt // A_BLK),
        in_specs=[pl.BlockSpec(memory_space=pltpu.SMEM),
                  pl.BlockSpec((A_BLK, qw), lambda h, b: (b, h)),
                  pl.BlockSpec((nt, HEAD_DIM), lambda h, b: (0, kcol + h)),
                  pl.BlockSpec((nt, HEAD_DIM), lambda h, b: (0, vcol + h))],
        out_specs=pl.BlockSpec((A_BLK, qw), lambda h, b: (b, h)),
        out_shape=jax.ShapeDtypeStruct((nt, A_HEADS * HEAD_DIM), BF16),
        compiler_params=_cp(("parallel", "arbitrary")),
        name="attn_a",
    )(sink, qkv, qkv, qkv)


def _na_bias_table(rel_bias):
    cq = jnp.arange(GRID_W)
    col_start = jnp.clip(cq - NA_KW // 2, 0, GRID_W - NA_KW)
    col_ok = (cq[None, :] >= col_start[:, None]) & (cq[None, :] < col_start[:, None] + NA_KW)
    dc = jnp.clip(cq[None, :] - cq[:, None], -(NA_KW - 1), NA_KW - 1) + (NA_KW - 1)
    b = rel_bias.astype(F32)[:, :, dc]
    b = jnp.where(col_ok[None, None], b, NEG_INF)
    rows = [jnp.concatenate([b[:, d0 + j] for j in range(NA_KH)], axis=-1) for d0 in range(NA_KH)]
    return jnp.stack(rows, axis=1)


def _na_kernel(q_ref, k_ref, v_ref, bw_ref, o_ref, *, nl, nc):
    qb = pl.program_id(1)
    scale = HEAD_DIM ** -0.5
    rows = nl // GRID_W
    rpb = nc // GRID_W
    kc = k_ref[nl:nl + nc, :].astype(BF16)
    vc = v_ref[nl:nl + nc, :].astype(BF16)

    @pl.when(qb == nl // nc)
    def _():
        q = q_ref[...].astype(BF16)
        s = _dot_nt(q, kc) * scale
        m = jnp.max(s, axis=-1, keepdims=True)
        p = jnp.exp(s - m)
        den = jnp.sum(p, axis=-1, keepdims=True)
        o_ref[...] = (_dot(p.astype(BF16), vc) / den).astype(o_ref.dtype)

    @pl.when(qb < nl // nc)
    def _():
        q_all = q_ref[...].astype(BF16)
        s_c_all = _dot_nt(q_all, kc) * scale
        rows_ = []
        for i in range(rpb):
            r = qb * rpb + i
            st = jnp.clip(r - NA_KH // 2, 0, rows - NA_KH)
            w0 = pl.multiple_of(st * GRID_W, GRID_W)
            rows_.append(dict(
                q=q_all[i * GRID_W:(i + 1) * GRID_W], s_c=s_c_all[i * GRID_W:(i + 1) * GRID_W],
                kw=k_ref[pl.ds(w0, NA_KH * GRID_W), :].astype(BF16),
                vw=v_ref[pl.ds(w0, NA_KH * GRID_W), :].astype(BF16),
                bias=bw_ref[st - r + (NA_KH - 1)]))
        for e in rows_:
            e["s_l"] = _dot_nt(e["q"], e["kw"]) * scale + e["bias"]
        for e in rows_:
            m = jnp.maximum(jnp.max(e["s_c"], axis=-1, keepdims=True), jnp.max(e["s_l"], axis=-1, keepdims=True))
            p_c = jnp.exp(e["s_c"] - m)
            p_l = jnp.exp(e["s_l"] - m)
            e["den"] = jnp.sum(p_c, axis=-1, keepdims=True) + jnp.sum(p_l, axis=-1, keepdims=True)
            e["p_c"] = p_c.astype(BF16)
            e["p_l"] = p_l.astype(BF16)
        y_c = _dot(jnp.concatenate([e["p_c"] for e in rows_], axis=0), vc)
        y_l = [_dot(e["p_l"], e["vw"]) for e in rows_]
        for i, e in enumerate(rows_):
            y = (y_c[i * GRID_W:(i + 1) * GRID_W] + y_l[i]) / e["den"]
            o_ref[i * GRID_W:(i + 1) * GRID_W, :] = y.astype(o_ref.dtype)


def _na(pc, bw, nl, nc, qcol):
    nt = pc.shape[0]
    qc = qcol // HEAD_DIM
    kcb = qc + D_HEADS
    vcb = qc + 2 * D_HEADS
    return pl.pallas_call(
        functools.partial(_na_kernel, nl=nl, nc=nc),
        grid=(D_HEADS, nt // nc),
        in_specs=[pl.BlockSpec((nc, HEAD_DIM), lambda h, b: (b, qc + h)),
                  pl.BlockSpec((nt, HEAD_DIM), lambda h, b: (0, kcb + h)),
                  pl.BlockSpec((nt, HEAD_DIM), lambda h, b: (0, vcb + h)),
                  pl.BlockSpec((None, NA_KH, GRID_W, NA_KH * GRID_W), lambda h, b: (h, 0, 0, 0))],
        out_specs=pl.BlockSpec((nc, HEAD_DIM), lambda h, b: (b, h)),
        out_shape=jax.ShapeDtypeStruct((nt, D_HEADS * HEAD_DIM), BF16),
        compiler_params=_cp(("parallel", "arbitrary")),
        name="na_attn",
    )(pc, pc, pc, bw)


def _conv3(x_ref, p_ref, n_ref, w_ref, scr, nb_lat):
    i = pl.program_id(0)
    r = x_ref.shape[0]
    x = x_ref[...]
    has_prev = (i >= 1) & (i < nb_lat)
    has_next = i < nb_lat - 1
    scr[8:8 + r, :] = x
    scr[7:8, :] = jnp.where(has_prev, p_ref[7:8, :], 0.0)
    scr[8 + r:9 + r, :] = jnp.where(has_next, n_ref[0:1, :], 0.0)
    w = w_ref[...]
    return w[0:1, :] * scr[7:7 + r, :] + w[1:2, :] * x + w[2:3, :] * scr[9:9 + r, :]


def _halo_specs(nc, tc, cb, nt):
    r8 = nc // 8
    last8 = nt // 8 - 1
    return [pl.BlockSpec((nc, tc), lambda i, j: (i, cb + j)),
            pl.BlockSpec((8, tc), lambda i, j: (jnp.maximum(i * r8 - 1, 0), cb + j)),
            pl.BlockSpec((8, tc), lambda i, j: (jnp.minimum((i + 1) * r8, last8), cb + j))]


def _dprep_kernel(x_ref, p_ref, n_ref, w_ref, o_ref, scr, *, nb_lat, n_norm):
    y = _silu(_conv3(x_ref, p_ref, n_ref, w_ref, scr, nb_lat))
    j = pl.program_id(1)
    tc = y.shape[1]
    for s in range(tc // HEAD_DIM):
        ys = y[:, s * HEAD_DIM:(s + 1) * HEAD_DIM]
        nrm = ys * lax.rsqrt(jnp.sum(ys * ys, axis=-1, keepdims=True) + NORM_EPS)
        o_ref[:, s * HEAD_DIM:(s + 1) * HEAD_DIM] = jnp.where(j < n_norm, nrm, ys)


def _dprep(pa, conv_w, nl, nc, col0):
    nt = pa.shape[0]
    tc = 512
    width = 3 * BW
    return pl.pallas_call(
        functools.partial(_dprep_kernel, nb_lat=nl // nc, n_norm=2 * BW // tc),
        grid=(nt // nc, width // tc),
        in_specs=_halo_specs(nc, tc, col0 // tc, nt) + [pl.BlockSpec((3, tc), lambda i, j: (0, j))],
        out_specs=pl.BlockSpec((nc, tc), lambda i, j: (i, j)),
        out_shape=jax.ShapeDtypeStruct((nt, width), F32),
        scratch_shapes=[pltpu.VMEM((nc + 16, tc), F32)],
        compiler_params=_cp(("parallel", "arbitrary")),
        name="delta_prep",
    )(pa, pa, pa, conv_w)


def _softplus(x):
    return jnp.maximum(x, 0.0) + jnp.log1p(jnp.exp(-jnp.abs(x)))


def _mxu(a, b, passes=1):
    ah = a.astype(BF16)
    bh = b.astype(BF16)
    out = _dot(ah, bh)
    if passes == 3:
        al = (a - ah.astype(F32)).astype(BF16)
        bl = (b - bh.astype(F32)).astype(BF16)
        out = out + _dot(al, bh) + _dot(ah, bl)
    return out


DELTA_SOLVE_PASSES = 1


def _delta_kernel(qf_ref, abf_ref, qb_ref, abb_ref, alog_ref, dtb_ref, of_ref, ob_ref, s_ref):
    ck = B_CHUNK
    nh = B_HEADS
    p = DELTA_SOLVE_PASSES

    @pl.when(pl.program_id(0) == 0)
    def _():
        s_ref[...] = jnp.zeros_like(s_ref)

    ri = lax.broadcasted_iota(I32, (ck, ck), 0)
    ci = lax.broadcasted_iota(I32, (ck, ck), 1)
    chains = []
    for d, (qkv_ref, ab_ref, o_ref) in enumerate(((qf_ref, abf_ref, of_ref), (qb_ref, abb_ref, ob_ref))):
        incl = (ri >= ci) if d == 0 else (ri <= ci)
        strict = (ri > ci) if d == 0 else (ri < ci)
        ab = ab_ref[...]
        dsl = slice(d * nh, (d + 1) * nh)
        g8 = -jnp.exp(alog_ref[:, dsl]) * _softplus(ab[:, dsl] + dtb_ref[:, dsl])
        b8 = jax.nn.sigmoid(ab[:, 2 * nh + d * nh:2 * nh + (d + 1) * nh])
        cum8 = _dot_hi(incl.astype(F32), g8)
        tot8 = jnp.sum(g8, axis=0, keepdims=True)
        for h in range(nh):
            gc = cum8[:, h:h + 1]
            gm = jnp.broadcast_to(gc, (ck, ck))
            chains.append(dict(
                qkv=qkv_ref, o=o_ref, h=h, si=d * nh + h, incl=incl, strict=strict, gc=gc, bc=b8[:, h:h + 1],
                tot=tot8[:, h:h + 1], decay=jnp.exp(jnp.where(incl, gm - gm.T, NEG_INF)), eg=jnp.exp(gc)))

    def head(c, part):
        off = part * BW + c["h"] * HEAD_DIM
        return c["qkv"][:, off:off + HEAD_DIM]

    for c in chains:
        k = head(c, 1)
        c["kb"] = k * c["bc"]
        c["kf"] = k.astype(BF16)
    for c in chains:
        c["a"] = jnp.where(c["strict"], _dot_nt(c["kb"].astype(BF16), c["kf"]) * c["decay"], 0.0)
    for c in chains:
        c["rhs"] = jnp.concatenate([head(c, 2) * c["bc"], c["kb"] * c["eg"]], axis=1)
    for c in chains:
        c["sol"] = c["rhs"] - _mxu(c["a"], c["rhs"], p)
        c["pw"] = c["a"]
    for _ in range(5):
        for c in chains:
            c["pw"] = _mxu(c["pw"], c["pw"], p)
        for c in chains:
            c["sol"] = c["sol"] + _mxu(c["pw"], c["sol"], p)
    for c in chains:
        c["q"] = head(c, 0) * (HEAD_DIM ** -0.5)
        c["qk"] = jnp.where(c["incl"], _dot_nt(c["q"].astype(BF16), c["kf"]) * c["decay"], 0.0)
    for c in chains:
        c["st"] = s_ref[c["si"]]
        c["stb"] = c["st"].astype(BF16)
        c["vn"] = (c["sol"][:, :HEAD_DIM] - _dot(c["sol"][:, HEAD_DIM:].astype(BF16), c["stb"])).astype(BF16)
    for c in chains:
        sl = slice(c["h"] * HEAD_DIM, (c["h"] + 1) * HEAD_DIM)
        c["o"][:, sl] = (_dot((c["q"] * c["eg"]).astype(BF16), c["stb"])
                         + _dot(c["qk"].astype(BF16), c["vn"]))
    for c in chains:
        kd = head(c, 1) * jnp.exp(c["tot"] - c["gc"])
        s_ref[c["si"]] = c["st"] * jnp.exp(c["tot"]) + _dot_tn(kd.astype(BF16), c["vn"])


def _delta(qkvb, pab, a_log, dt_bias, nl, nc):
    nt = qkvb.shape[0]
    ck = B_CHUNK
    nlc, ncc = nl // ck, nc // ck
    ntc = nlc + ncc
    rf = lambda c: jnp.where(c < ncc, nlc + c, c - ncc)
    rb = lambda c: jnp.where(c < ncc, nlc + ncc - 1 - c, nlc - 1 - (c - ncc))
    vec = pl.BlockSpec((1, 2 * B_HEADS), lambda c: (0, 0))
    out = jax.ShapeDtypeStruct((nt, BW), F32)
    return pl.pallas_call(
        _delta_kernel,
        grid=(ntc,),
        in_specs=[pl.BlockSpec((ck, 3 * BW), lambda c: (rf(c), 0)),
                  pl.BlockSpec((ck, pab.shape[1]), lambda c: (rf(c), 0)),
                  pl.BlockSpec((ck, 3 * BW), lambda c: (rb(c), 0)),
                  pl.BlockSpec((ck, pab.shape[1]), lambda c: (rb(c), 0)),
                  vec, vec],
        out_specs=[pl.BlockSpec((ck, BW), lambda c: (rf(c), 0)), pl.BlockSpec((ck, BW), lambda c: (rb(c), 0))],
        out_shape=[out, out],
        scratch_shapes=[pltpu.VMEM((2 * B_HEADS, HEAD_DIM, HEAD_DIM), F32)],
        compiler_params=_cp(("arbitrary",)),
        name="delta_scan",
    )(qkvb, pab, qkvb, pab, a_log.reshape(1, 2 * B_HEADS), dt_bias.reshape(1, 2 * B_HEADS))


def _dpost_kernel(of_ref, ob_ref, g_ref, ng_ref, o_ref):
    o = of_ref[...] + ob_ref[...]
    gate = g_ref[...]
    ng = ng_ref[...]
    for s in range(o.shape[1] // HEAD_DIM):
        sl = slice(s * HEAD_DIM, (s + 1) * HEAD_DIM)
        os_ = o[:, sl]
        y = os_ * lax.rsqrt(jnp.mean(os_ * os_, axis=-1, keepdims=True) + NORM_EPS) * ng
        o_ref[:, sl] = (y * _silu(gate[:, sl])).astype(o_ref.dtype)


def _dpost(o_f, o_b, pa, norm_g, nc, gcol):
    nt = pa.shape[0]
    tc = 512
    gb = gcol // tc
    return pl.pallas_call(
        _dpost_kernel,
        grid=(nt // nc, BW // tc),
        in_specs=[pl.BlockSpec((nc, tc), lambda i, j: (i, j)),
                  pl.BlockSpec((nc, tc), lambda i, j: (i, j)),
                  pl.BlockSpec((nc, tc), lambda i, j: (i, gb + j)),
                  pl.BlockSpec((1, HEAD_DIM), lambda i, j: (0, 0))],
        out_specs=pl.BlockSpec((nc, tc), lambda i, j: (i, j)),
        out_shape=jax.ShapeDtypeStruct((nt, BW), BF16),
        compiler_params=_cp(("parallel", "parallel")),
        name="delta_post",
    )(o_f, o_b, pa, norm_g.reshape(1, HEAD_DIM))


def _hconv_kernel(a_ref, ap_ref, an_ref, b_ref, bp_ref, bn_ref, c_ref, cp_ref, cn_ref,
                  wa_ref, wb_ref, wc_ref, x0_ref, z_ref, scr, *, nb_lat):
    x0_ref[...] = _conv3(a_ref, ap_ref, an_ref, wa_ref, scr, nb_lat)
    x1 = _conv3(b_ref, bp_ref, bn_ref, wb_ref, scr, nb_lat)
    v = _conv3(c_ref, cp_ref, cn_ref, wc_ref, scr, nb_lat)
    z_ref[...] = v * x1


def _hconv(pc, conv_w, nl, nc):
    nt = pc.shape[0]
    tc = 512
    gpb = BW // tc
    specs = []
    for g in range(3):
        specs += _halo_specs(nc, tc, g * gpb, nt)
    for g in range(3):
        specs.append(pl.BlockSpec((3, tc), lambda i, j, g=g: (0, g * gpb + j)))
    out = pl.BlockSpec((nc, tc), lambda i, j: (i, j))
    return pl.pallas_call(
        functools.partial(_hconv_kernel, nb_lat=nl // nc),
        grid=(nt // nc, gpb),
        in_specs=specs,
        out_specs=[out, out],
        out_shape=[jax.ShapeDtypeStruct((nt, BW), F32)] * 2,
        scratch_shapes=[pltpu.VMEM((nc + 16, tc), F32)],
        compiler_params=_cp(("parallel", "arbitrary")),
        name="hyena_conv",
    )(pc, pc, pc, pc, pc, pc, pc, pc, pc, conv_w, conv_w, conv_w)


def _hfilter_features(n):
    t = np.linspace(0.0, 1.0, n, dtype=np.float32)[:, None]
    bands = (HYENA_EMB - 1) // 2
    wpos = (2.0 * math.pi * np.arange(n, dtype=np.float32)[:, None] / n).astype(np.float32)
    f = np.linspace(1e-4, bands - 1, bands, dtype=np.float32)[None, :]
    z = np.concatenate([t, np.cos(f * wpos), -np.sin(f * wpos)], axis=-1).astype(np.float32)
    pos = np.concatenate([np.arange(n), [0], 2 * n - np.arange(n + 1, 2 * n)])
    zk = np.zeros((2 * n, LANES), np.float32)
    zk[:, :HYENA_EMB] = z[pos]
    max_decay = math.log(1e-2) / 0.3
    min_decay = math.log(1e-2) / 1.5
    absd = np.abs(np.linspace(min_decay, max_decay, BW, dtype=np.float32))[None, :]
    return jnp.asarray(zk), jnp.asarray(absd)


def _hfilter_kernel(z_ref, w1, b1, fr, w2, b2, w3, b3, w4, b4, ad, o_ref, *, n, tr):
    z = z_ref[...]
    frq = fr[...]
    h = jnp.sin(frq * (_mxu(z, w1[...], 3) + b1[...]))
    h = jnp.sin(frq * (_mxu(h, w2[...], 3) + b2[...]))
    h = jnp.sin(frq * (_mxu(h, w3[...], 3) + b3[...]))
    h = _mxu(h, w4[...]) + b4[...]
    win = jnp.exp(-z[:, 0:1] * ad[...])
    row = pl.program_id(0) * tr + lax.broadcasted_iota(I32, (tr, 1), 0)
    o_ref[...] = jnp.where(row == n, 0.0, h * win)


def _hfilter(n, zk, absd, w1, b1, fr, w2, b2, w3, b3, w4, b4):
    tr = min(512, n)
    nfb = n // tr
    w1p = jnp.zeros((LANES, HYENA_FFN), F32).at[:HYENA_EMB].set(w1)
    small = lambda shp: pl.BlockSpec(shp, lambda i: (0, 0))
    row = lambda a: a.reshape(1, -1)
    return pl.pallas_call(
        functools.partial(_hfilter_kernel, n=n, tr=tr),
        grid=(2 * n // tr,),
        in_specs=[pl.BlockSpec((tr, LANES), lambda i: (i, 0)),
                  small((LANES, HYENA_FFN)), small((1, HYENA_FFN)), small((1, HYENA_FFN)),
                  small((HYENA_FFN, HYENA_FFN)), small((1, HYENA_FFN)),
                  small((HYENA_FFN, HYENA_FFN)), small((1, HYENA_FFN)),
                  pl.BlockSpec((HYENA_FFN, BW), lambda i: (0, (i >= nfb).astype(I32))),
                  pl.BlockSpec((1, BW), lambda i: (0, (i >= nfb).astype(I32))),
                  small((1, BW))],
        out_specs=pl.BlockSpec((tr, BW), lambda i: (i, 0)),
        out_shape=jax.ShapeDtypeStruct((2 * n, BW), F32),
        compiler_params=_cp(("parallel",)),
        name="hyena_filter",
    )(zk, w1p, row(b1), row(fr), w2, row(b2), w3, row(b3), w4, row(b4), absd)


def _fft_tables(n):
    big = 2 * n
    h = big // FFT_L
    l2 = FFT_L // 2

    def cs(a, b, period):
        ang = 2.0 * np.pi * np.outer(np.arange(a), np.arange(b)) / period
        return np.cos(ang), np.sin(ang)

    ch, sh = cs(h, h, h)
    cl, sl = cs(FFT_L, FFT_L, FFT_L)
    ct, st = cs(h, FFT_L, big)
    f = lambda a: jnp.asarray(a, F32)
    return dict(
        fhr=f(ch), fhi=f(-sh),
        flr=f(cl[:l2]), fli=f(-sl[:l2]),
        twr=f(ct), twi=f(-st),
        clr=f(cl[:, :l2]), cli=f(sl[:, :l2]),
        chr=f(ch[:h // 2]), chi=f(sh[:h // 2]),
        ctr=f(ct.T), cti=f(st.T),
    )


def _alt_sum(x):
    row = lax.broadcasted_iota(I32, x.shape, 0)
    return jnp.sum(jnp.where((row & 1) == 1, -x, x), axis=0, keepdims=True)


FFT_PITCH = FFT_L + 8
FFT_GROUP = 8


def _lane_halves(x):
    return x[:, :LANES], x[:, LANES:]


def _fft_stage1(x_ref, vr_ref, vi_ref, fhr_ref, fhi_ref, h_in, h):
    fs = jnp.concatenate([fhr_ref[:, 0:h_in], fhi_ref[:, 0:h_in]], axis=0).astype(BF16)
    g = FFT_GROUP

    def body(step, carry):
        m0 = step * g
        cols = [x_ref[pl.ds(m0 + i, h_in, stride=FFT_L), :].astype(BF16) for i in range(g)]
        outs = [_dot(fs, jnp.concatenate(cols[i:i + 2], axis=1)) for i in range(0, g, 2)]
        for i in range(g):
            o = _lane_halves(outs[i // 2])[i % 2]
            vr_ref[pl.ds(m0 + i, h, stride=FFT_PITCH), :] = o[:h]
            vi_ref[pl.ds(m0 + i, h, stride=FFT_PITCH), :] = o[h:]
        return carry

    lax.fori_loop(0, FFT_L // g, body, 0)


def _fft_stage2_blocks(f0, n_blk, vr_ref, vi_ref, flr_ref, fli_ref, twr_ref, twi_ref):
    flr = flr_ref[...]
    fli = fli_ref[...]
    mats, ops = [], []
    for i in range(n_blk):
        twr = twr_ref[pl.ds(f0 + i, 1), :]
        twi = twi_ref[pl.ds(f0 + i, 1), :]
        mats.append(((flr * twr - fli * twi).astype(BF16), (flr * twi + fli * twr).astype(BF16)))
        r0 = pl.multiple_of((f0 + i) * FFT_PITCH, 8)
        ops.append(jnp.concatenate([vr_ref[pl.ds(r0, FFT_L), :], vi_ref[pl.ds(r0, FFT_L), :]],
                                   axis=1).astype(BF16))
    pa = [_dot(m[0], v) for m, v in zip(mats, ops)]
    pb = [_dot(m[1], v) for m, v in zip(mats, ops)]
    out = []
    for a, b in zip(pa, pb):
        ar, ai = _lane_halves(a)
        br, bi = _lane_halves(b)
        out.append((ar - bi, ai + br))
    return out


def _kfft_kernel(k_ref, fhr, fhi, flr, fli, twr, twi, kr_ref, ki_ref, kn_ref, vr_ref, vi_ref, *, h):
    l2 = FFT_L // 2
    g = min(FFT_GROUP, h)
    kn_ref[...] = _alt_sum(k_ref[...])
    _fft_stage1(k_ref, vr_ref, vi_ref, fhr, fhi, h, h)

    def body(step, carry):
        f0 = step * g
        for i, (xr, xi) in enumerate(_fft_stage2_blocks(f0, g, vr_ref, vi_ref, flr, fli, twr, twi)):
            k0 = pl.multiple_of((f0 + i) * l2, l2)
            kr_ref[pl.ds(k0, l2), :] = xr
            ki_ref[pl.ds(k0, l2), :] = xi
        return carry

    lax.fori_loop(0, h // g, body, 0)


def _table_specs(tabs, names):
    return [pl.BlockSpec(tabs[k].shape, lambda j: (0, 0)) for k in names]


def _kfft(kt, tabs, n):
    big = 2 * n
    h = big // FFT_L
    l2 = FFT_L // 2
    names = ["fhr", "fhi", "flr", "fli", "twr", "twi"]
    spec = pl.BlockSpec((h * l2, LANES), lambda j: (0, j))
    return pl.pallas_call(
        functools.partial(_kfft_kernel, h=h),
        grid=(BW // LANES,),
        in_specs=[pl.BlockSpec((big, LANES), lambda j: (0, j))] + _table_specs(tabs, names),
        out_specs=[spec, spec, pl.BlockSpec((1, LANES), lambda j: (0, j))],
        out_shape=[jax.ShapeDtypeStruct((h * l2, BW), F32)] * 2 + [jax.ShapeDtypeStruct((1, BW), F32)],
        scratch_shapes=[pltpu.VMEM((h * FFT_PITCH, LANES), F32)] * 2,
        compiler_params=_cp(("parallel",)),
        name="hyena_filter_fft",
    )(kt, *[tabs[k] for k in names])


def _zfft_kernel(z_ref, x0_ref, kr_ref, ki_ref, kn_ref, d_ref, fhr, fhi, flr, fli, twr, twi,
                 clr, cli, chr_, chi, ctr, cti, o_ref, vr_ref, vi_ref, *, h):
    l2 = FFT_L // 2
    hh = h // 2
    big = h * FFT_L
    pn = _alt_sum(z_ref[...]) * kn_ref[...] * (1.0 / big)
    _fft_stage1(z_ref, vr_ref, vi_ref, fhr, fhi, hh, h)
    cr = clr[...].astype(BF16)
    ci = cli[...].astype(BF16)
    rowi = lax.broadcasted_iota(I32, (l2, 1), 0)
    gm = min(FFT_GROUP // 2, h)

    def mid(step, carry):
        f0 = step * gm
        prods = []
        for i, (xr, xi) in enumerate(_fft_stage2_blocks(f0, gm, vr_ref, vi_ref, flr, fli, twr, twi)):
            k0 = pl.multiple_of((f0 + i) * l2, l2)
            kr = kr_ref[pl.ds(k0, l2), :]
            ki = ki_ref[pl.ds(k0, l2), :]
            half_dc = jnp.where((rowi == 0) & (f0 + i == 0), 0.5, 1.0)
            prods.append(jnp.concatenate([(xr * kr - xi * ki) * half_dc, (xr * ki + xi * kr) * half_dc],
                                         axis=1).astype(BF16))
        za = [_dot(cr, p) for p in prods]
        zb = [_dot(ci, p) for p in prods]
        for i in range(gm):
            ar, ai = _lane_halves(za[i])
            br, bi = _lane_halves(zb[i])
            r0 = pl.multiple_of((f0 + i) * FFT_PITCH, 8)
            vr_ref[pl.ds(r0, FFT_L), :] = ar - bi
            vi_ref[pl.ds(r0, FFT_L), :] = ai + br
        return carry

    lax.fori_loop(0, h // gm, mid, 0)
    ch_r = chr_[...]
    ch_i = chi[...]
    dvec = d_ref[...]
    gl = FFT_GROUP

    def last(step, carry):
        m0 = step * gl
        mats, ops = [], []
        for i in range(gl):
            tr_ = ctr[pl.ds(m0 + i, 1), :]
            ti_ = cti[pl.ds(m0 + i, 1), :]
            mats.append(jnp.concatenate([ch_r * tr_ - ch_i * ti_, -(ch_r * ti_ + ch_i * tr_)], axis=1).astype(BF16))
            ops.append(jnp.concatenate([vr_ref[pl.ds(m0 + i, h, stride=FFT_PITCH), :],
                                        vi_ref[pl.ds(m0 + i, h, stride=FFT_PITCH), :]], axis=0).astype(BF16))
        ys = [_dot(m, v) for m, v in zip(mats, ops)]
        for i in range(gl):
            sgn = jnp.where(((m0 + i) & 1) == 1, -1.0, 1.0)
            y = ys[i] * (2.0 / big) + pn * sgn
            zt = z_ref[pl.ds(m0 + i, hh, stride=FFT_L), :]
            x0 = x0_ref[pl.ds(m0 + i, hh, stride=FFT_L), :]
            o_ref[pl.ds(m0 + i, hh, stride=FFT_L), :] = (y + dvec * zt) * x0
        return carry

    lax.fori_loop(0, FFT_L // gl, last, 0)


def _zfft(z, x0c, kr, ki, kn, d_bias, tabs, nl):
    big = 2 * nl
    h = big // FFT_L
    l2 = FFT_L // 2
    names = ["fhr", "fhi", "flr", "fli", "twr", "twi", "clr", "cli", "chr", "chi", "ctr", "cti"]
    one = pl.Buffered(1)
    col = lambda rows: pl.BlockSpec((rows, LANES), lambda j: (0, j), pipeline_mode=one)
    vec = pl.BlockSpec((1, LANES), lambda j: (0, j))
    return pl.pallas_call(
        functools.partial(_zfft_kernel, h=h),
        grid=(BW // LANES,),
        in_specs=[col(nl), col(nl), col(h * l2), col(h * l2), vec, vec] + _table_specs(tabs, names),
        out_specs=pl.BlockSpec((nl, LANES), lambda j: (0, j)),
        out_shape=jax.ShapeDtypeStruct((nl, BW), F32),
        scratch_shapes=[pltpu.VMEM((h * FFT_PITCH, LANES), F32)] * 2,
        compiler_params=_cp(("parallel",)),
        name="hyena_fftconv",
    )(z, x0c, kr, ki, kn, d_bias.reshape(1, BW), *[tabs[k] for k in names])


def _hsmall_kernel(z_ref, x0_ref, k_ref, d_ref, fr_ref, fi_ref, o_ref, *, nc):
    big = 2 * nc
    fr = fr_ref[...].astype(BF16)
    fi = fi_ref[...].astype(BF16)
    z = z_ref[...]
    zb = z.astype(BF16)
    kb = k_ref[...].astype(BF16)
    xr = _dot(fr[:, :nc], zb)
    xi = _dot(fi[:, :nc], zb)
    kr = _dot(fr, kb)
    ki = _dot(fi, kb)
    pr = (xr * kr - xi * ki).astype(BF16)
    pi = (xr * ki + xi * kr).astype(BF16)
    y = (_dot(fr[:nc, :], pr) + _dot(fi[:nc, :], pi)) * (1.0 / big)
    o_ref[...] = (y + d_ref[...] * z) * x0_ref[...]


def _hsmall(z, x0c, kt_c, d_bias, nl, nc):
    big = 2 * nc
    ang = 2.0 * np.pi * np.outer(np.arange(big), np.arange(big)) / big
    fr = jnp.asarray(np.cos(ang), F32)
    fi = jnp.asarray(-np.sin(ang), F32)
    tc = 256
    rb = nl // nc
    return pl.pallas_call(
        functools.partial(_hsmall_kernel, nc=nc),
        grid=(BW // tc,),
        in_specs=[pl.BlockSpec((nc, tc), lambda j: (rb, j)),
                  pl.BlockSpec((nc, tc), lambda j: (rb, j)),
                  pl.BlockSpec((big, tc), lambda j: (0, j)),
                  pl.BlockSpec((1, tc), lambda j: (0, j)),
                  pl.BlockSpec((big, big), lambda j: (0, 0)),
                  pl.BlockSpec((big, big), lambda j: (0, 0))],
        out_specs=pl.BlockSpec((nc, tc), lambda j: (0, j)),
        out_shape=jax.ShapeDtypeStruct((nc, BW), F32),
        compiler_params=_cp(("parallel",)),
        name="hyena_ctx",
    )(z, x0c, kt_c, d_bias.reshape(1, BW), fr, fi)


def _merge_kernel(ya, yb, ycl, ycc, yd, g0, g1, g2, g3, wb_ref, o_ref):
    is_ctx = pl.program_id(1) == pl.num_programs(1) - 1
    yc = jnp.where(is_ctx, ycc[...], ycl[...])
    acc = None
    for b, (y, g) in enumerate(((ya[...], g0), (yb[...], g1), (yc, g2), (yd[...], g3))):
        p = _dot(y.astype(BF16), wb_ref[b].astype(BF16))
        t = jax.nn.sigmoid(g[...].astype(F32)) * p
        acc = t if acc is None else acc + t
    o_ref[...] = acc.astype(o_ref.dtype)


def _merge(ya, yb, yc_lat, yc_ctx, yd, pc, w_branch, l, gcol, nc):
    nt = ya.shape[0]
    tn = 512
    d = D_MODEL
    nb_lat = yc_lat.shape[0] // nc
    ysp = pl.BlockSpec((nc, BW), lambda j, i: (i, 0))
    gsp = [pl.BlockSpec((nc, tn), lambda j, i, b=b: (i, (gcol + b * d) // tn + j)) for b in range(4)]
    return pl.pallas_call(
        _merge_kernel,
        grid=(d // tn, nt // nc),
        in_specs=[ysp, ysp,
                  pl.BlockSpec((nc, BW), lambda j, i: (jnp.minimum(i, nb_lat - 1), 0)),
                  pl.BlockSpec((nc, BW), lambda j, i: (0, 0)),
                  ysp] + gsp + [pl.BlockSpec((None, 4, BW, tn), lambda j, i: (l, 0, 0, j))],
        out_specs=pl.BlockSpec((nc, tn), lambda j, i: (i, j)),
        out_shape=jax.ShapeDtypeStruct((nt, d), BF16),
        compiler_params=_cp(("parallel", "arbitrary")),
        name="merge",
    )(ya, yb, yc_lat, yc_ctx, yd, pc, pc, pc, pc, w_branch)


def _post_norm(x, delta, gate, lng, lnb):
    y = ALPHA * x + gate * delta
    mu = jnp.mean(y, axis=-1, keepdims=True)
    yc = y - mu
    var = jnp.mean(yc * yc, axis=-1, keepdims=True)
    return yc * lax.rsqrt(var + LN_EPS) * lng + lnb


def _ln1_kernel(x_ref, dl_ref, g_ref, lng_ref, lnb_ref, sh_ref, sc_ref, wr_ref, xo_ref, uo_ref, lg_ref):
    is_ctx = pl.program_id(0) == pl.num_programs(0) - 1
    xn = _post_norm(x_ref[...], dl_ref[...], _pick_mod(g_ref[...], is_ctx), lng_ref[...], lnb_ref[...])
    xo_ref[...] = xn
    u = xn * (1.0 + _pick_mod(sc_ref[...], is_ctx)) + _pick_mod(sh_ref[...], is_ctx)
    uo_ref[...] = u
    lg_ref[...] = _dot_nt_hi(wr_ref[...], u)


def _ln1(x, mix, mod_l, lng, lnb, w_router, nc):
    nt, d = x.shape
    row = pl.BlockSpec((nc, d), lambda i: (i, 0))
    modc = lambda c: pl.BlockSpec((8, d), lambda i: (0, c))
    vec = pl.BlockSpec((1, d), lambda i: (0, 0))
    return pl.pallas_call(
        _ln1_kernel,
        grid=(nt // nc,),
        in_specs=[row, row, modc(2), vec, vec, modc(3), modc(4),
                  pl.BlockSpec((N_EXPERTS, d), lambda i: (0, 0))],
        out_specs=[row, row, pl.BlockSpec((N_EXPERTS, nc), lambda i: (0, i))],
        out_shape=[jax.ShapeDtypeStruct((nt, d), F32), jax.ShapeDtypeStruct((nt, d), F32),
                   jax.ShapeDtypeStruct((N_EXPERTS, nt), F32)],
        compiler_params=_cp(("parallel",)),
        name="ln1_router",
    )(x, mix, mod_l, lng.reshape(1, d), lnb.reshape(1, d), mod_l, mod_l, w_router.T)


def _cumsum_lanes(x):
    n = x.shape[1]
    lane = lax.broadcasted_iota(I32, x.shape, 1)
    s = 1
    while s < n:
        x = x + jnp.where(lane >= s, pltpu.roll(x, s, 1), 0.0)
        s *= 2
    return x


def _select_kernel(lg_ref, idx_ref, w_ref, posx_ref, posm_ref, aff_scr, pos_scr, *, n, cap, sc):
    lg = lg_ref[...]
    e = jnp.exp(lg - jnp.max(lg, axis=0, keepdims=True))
    aff = e / jnp.sum(e, axis=0, keepdims=True)
    bits = pltpu.bitcast(aff, I32)
    tau = jnp.zeros((N_EXPERTS, 1), I32)
    for bit in range(30, -1, -1):
        cand = tau | (1 << bit)
        cnt = jnp.sum((bits >= cand).astype(F32), axis=1, keepdims=True)
        tau = jnp.where(cnt >= cap, cand, tau)
    gt = bits > tau
    eq = (bits == tau).astype(F32)
    need = cap - jnp.sum(gt.astype(F32), axis=1, keepdims=True)
    eq_rank = _cumsum_lanes(eq) - eq
    sel = jnp.where(gt, 1.0, jnp.where(eq_rank < need, eq, 0.0))
    posx = _cumsum_lanes(sel) - sel
    posx_ref[...] = posx.astype(I32)
    posm = jnp.where(sel > 0.0, posx, -1.0)
    posm_ref[...] = posm.astype(I32)
    aff_scr[...] = aff
    pos_scr[...] = posm
    nch = cap // sc
    tok = lax.broadcasted_iota(I32, (sc, n), 1).astype(F32)
    slot0 = lax.broadcasted_iota(I32, (sc, 1), 0).astype(F32)
    lane = lax.broadcasted_iota(I32, (sc, LANES), 1)

    def per_expert(ex, carry):
        it, wt = carry
        prow = pos_scr[pl.ds(ex, 1), :]
        arow = aff_scr[pl.ds(ex, 1), :]
        for ch in range(nch):
            hit = prow == (slot0 + float(ch * sc))
            icol = jnp.sum(jnp.where(hit, tok, 0.0), axis=1, keepdims=True)
            wcol = jnp.sum(jnp.where(hit, arow, 0.0), axis=1, keepdims=True)
            here = lane == ex * nch + ch
            it = jnp.where(here, icol, it)
            wt = jnp.where(here, wcol, wt)
        return it, wt

    zero = jnp.zeros((sc, LANES), F32)
    it, wt = lax.fori_loop(0, N_EXPERTS, per_expert, (zero, zero))
    idx_ref[...] = it.astype(I32)
    w_ref[...] = wt


def _select(lg_t, col0, n, cap):
    sc = min(LANES, cap)
    nch = cap // sc
    assert nch * N_EXPERTS <= LANES and col0 % n == 0
    idx_t, w_t, posx, posm = pl.pallas_call(
        functools.partial(_select_kernel, n=n, cap=cap, sc=sc),
        grid=(1,),
        in_specs=[pl.BlockSpec((N_EXPERTS, n), lambda i: (0, col0 // n))],
        out_specs=[pl.BlockSpec((sc, LANES), lambda i: (0, 0)), pl.BlockSpec((sc, LANES), lambda i: (0, 0)),
                   pl.BlockSpec((N_EXPERTS, n), lambda i: (0, 0)), pl.BlockSpec((N_EXPERTS, n), lambda i: (0, 0))],
        out_shape=[jax.ShapeDtypeStruct((sc, LANES), I32), jax.ShapeDtypeStruct((sc, LANES), F32),
                   jax.ShapeDtypeStruct((N_EXPERTS, n), I32), jax.ShapeDtypeStruct((N_EXPERTS, n), I32)],
        scratch_shapes=[pltpu.VMEM((N_EXPERTS, n), F32)] * 2,
        compiler_params=_cp(("arbitrary",)),
        name="moe_select",
    )(lg_t)
    unt = lambda a: a[:, :N_EXPERTS * nch].T.reshape(N_EXPERTS, cap)
    return unt(idx_t), unt(w_t), posx, posm


def _ffn_kernel(idx_ref, u_hbm, ws_ref, w1_ref, w3_ref, w2_ref, o_ref, xg, xb, acc, sem, *, capt, capp):
    ex = pl.program_id(0)
    f = pl.program_id(1)

    def row_copy(r, src_row):
        return pltpu.make_async_copy(u_hbm.at[pl.ds(src_row, 1)], xg.at[pl.ds(r, 1)], sem)

    def gather(which):
        def issue(r, carry):
            row_copy(r, idx_ref[which, r]).start()
            return carry

        lax.fori_loop(0, capt, issue, 0, unroll=8)

    @pl.when(f == 0)
    def _():
        @pl.when(ex == 0)
        def _():
            gather(ex)

        def drain(r, carry):
            row_copy(r, 0).wait()
            return carry

        lax.fori_loop(0, capt, drain, 0, unroll=8)
        xb[...] = xg[...].astype(BF16)

        @pl.when(ex + 1 < pl.num_programs(0))
        def _():
            gather(ex + 1)

        acc[...] = jnp.zeros_like(acc)

    x = xb[...]
    hdn = _silu(_dot(x, w1_ref[...].astype(BF16))) * _dot(x, w3_ref[...].astype(BF16))
    acc[...] += _dot(hdn.astype(BF16), w2_ref[...].astype(BF16))

    @pl.when(f == pl.num_programs(1) - 1)
    def _():
        o_ref[0:capt, :] = (acc[...] * ws_ref[...]).astype(o_ref.dtype)
        o_ref[capt:capp, :] = jnp.zeros((capp - capt, o_ref.shape[1]), o_ref.dtype)


def _ffn(idx, u2, wsel, w_e1, w_e3, w_e2, l, capp):
    capt = idx.shape[1]
    d = D_MODEL
    tf = 256
    grid_spec = pltpu.PrefetchScalarGridSpec(
        num_scalar_prefetch=1,
        grid=(N_EXPERTS, EXPERT_FF // tf),
        in_specs=[pl.BlockSpec(memory_space=pl.ANY),
                  pl.BlockSpec((None, capt, 1), lambda e, f, idx: (e, 0, 0)),
                  pl.BlockSpec((None, None, d, tf), lambda e, f, idx: (l, e, 0, f)),
                  pl.BlockSpec((None, None, d, tf), lambda e, f, idx: (l, e, 0, f)),
                  pl.BlockSpec((None, None, tf, d), lambda e, f, idx: (l, e, f, 0))],
        out_specs=pl.BlockSpec((None, capp, d), lambda e, f, idx: (e, 0, 0)),
        scratch_shapes=[pltpu.VMEM((capt, d), F32), pltpu.VMEM((capt, d), BF16), pltpu.VMEM((capt, d), F32),
                        pltpu.SemaphoreType.DMA(())],
    )
    return pl.pallas_call(
        functools.partial(_ffn_kernel, capt=capt, capp=capp),
        grid_spec=grid_spec,
        out_shape=jax.ShapeDtypeStruct((N_EXPERTS, capp, d), BF16),
        compiler_params=_cp(("arbitrary", "arbitrary")),
        name="moe_ffn",
    )(idx, u2, wsel[..., None], w_e1, w_e3, w_e2)


COMB_CHUNK = 128
COMB_ALIGN = 16


def _combine_kernel(st_ref, nch_ref, posm_ref, y_hbm, x_ref, g_ref, lng_ref, lnb_ref, sh_ref, sc_ref,
                    xo_ref, uo_ref, ybuf, yext, acc_ref, sem, sem_x):
    b = pl.program_id(0)
    nb = pl.num_programs(0)
    is_ctx = b == nb - 1
    nc = x_ref.shape[0]
    ch = COMB_CHUNK

    def base_of(ex, blk):
        s0 = st_ref[ex, blk]
        return pl.multiple_of(s0 - (s0 & (COMB_ALIGN - 1)), COMB_ALIGN)

    def first_copy(ex, blk, slot):
        return pltpu.make_async_copy(y_hbm.at[ex, pl.ds(base_of(ex, blk), ch)],
                                     ybuf.at[slot, pl.ds(ex * ch, ch)], sem.at[slot])

    def issue(blk, slot):
        for ex in range(N_EXPERTS):
            first_copy(ex, blk, slot).start()

    slot = b % 2

    @pl.when(b == 0)
    def _():
        issue(b, slot)

    @pl.when(b + 1 < nb)
    def _():
        issue(b + 1, 1 - slot)

    for ex in range(N_EXPERTS):
        first_copy(ex, b, slot).wait()
    posm = posm_ref[...]
    lane = lax.broadcasted_iota(I32, (nc, ch), 1)
    rels = [posm[:, ex:ex + 1] - base_of(ex, b) for ex in range(N_EXPERTS)]
    onehot = jnp.concatenate([(r == lane).astype(BF16) for r in rels], axis=1)
    acc_ref[...] = _dot(onehot, ybuf[slot])
    for ex in range(N_EXPERTS):
        def extra(j, carry, ex=ex):
            cp = pltpu.make_async_copy(y_hbm.at[ex, pl.ds(base_of(ex, b) + j * ch, ch)], yext, sem_x)
            cp.start()
            cp.wait()
            acc_ref[...] += _dot(((rels[ex] - j * ch) == lane).astype(BF16), yext[...])
            return carry

        lax.fori_loop(1, nch_ref[ex, b], extra, 0)
    xn = _post_norm(x_ref[...], acc_ref[...], _pick_mod(g_ref[...], is_ctx), lng_ref[...], lnb_ref[...])
    xo_ref[...] = xn
    u = xn * (1.0 + _pick_mod(sc_ref[...], is_ctx)) + _pick_mod(sh_ref[...], is_ctx)
    uo_ref[...] = u.astype(uo_ref.dtype)


def _combine(starts, counts, posm, y, x1, mod_l, lng, lnb, mod_next, nc):
    nt, d = x1.shape
    nch = (((starts & (COMB_ALIGN - 1)) + counts + COMB_CHUNK - 1) // COMB_CHUNK).astype(I32)
    row = lambda: pl.BlockSpec((nc, d), lambda i, st, nq: (i, 0))
    modc = lambda c: pl.BlockSpec((8, d), lambda i, st, nq: (0, c))
    vec = pl.BlockSpec((1, d), lambda i, st, nq: (0, 0))
    grid_spec = pltpu.PrefetchScalarGridSpec(
        num_scalar_prefetch=2,
        grid=(nt // nc,),
        in_specs=[pl.BlockSpec((nc, N_EXPERTS), lambda i, st, nq: (i, 0)),
                  pl.BlockSpec(memory_space=pl.ANY),
                  row(), modc(5), vec, vec, modc(0), modc(1)],
        out_specs=[row(), row()],
        scratch_shapes=[pltpu.VMEM((2, N_EXPERTS * COMB_CHUNK, d), BF16), pltpu.VMEM((COMB_CHUNK, d), BF16),
                        pltpu.VMEM((nc, d), F32), pltpu.SemaphoreType.DMA((2,)), pltpu.SemaphoreType.DMA(())],
    )
    return pl.pallas_call(
        _combine_kernel,
        grid_spec=grid_spec,
        out_shape=[jax.ShapeDtypeStruct((nt, d), F32), jax.ShapeDtypeStruct((nt, d), BF16)],
        compiler_params=_cp(("arbitrary",)),
        name="moe_combine_ln2",
    )(starts, nch, posm.T, y, x1, mod_l, lng.reshape(1, d), lnb.reshape(1, d), mod_next, mod_next)


def _row_tile(nt, nc):
    return nt // 4 if (nt % 4 == 0 and (nt // 4) % 16 == 0 and nt // 4 >= nc) else nc


def kernel(x, c, ctx, c_ctx, w_mod, b_mod, w_in, attn_sink, delta_conv, delta_a_log, delta_dt_bias, delta_norm_g, hyena_conv, hyena_w1, hyena_b1, hyena_freq, hyena_w2, hyena_b2, hyena_w3, hyena_b3, hyena_w4, hyena_b4, hyena_d, na_rel_bias, w_branch, w_o, ln1_g, ln1_b, ln2_g, ln2_b, w_router, w_e1, w_e3, w_e2):
    nl, nc = x.shape[1], ctx.shape[1]
    nt = nl + nc
    depth = w_in.shape[0]
    tm = _row_tile(nt, nc)
    cap_l = EC_CAPACITY * nl // N_EXPERTS
    cap_c = EC_CAPACITY * nc // N_EXPERTS
    capp = cap_l + cap_c + COMB_CHUNK

    xc = jnp.concatenate([x[0], ctx[0]], axis=0)
    mod = _modvec(c, c_ctx, w_mod, b_mod)
    cos_t, sin_t = _rope_tables(nl, nc)
    tabs = _fft_tables(nl)
    zk_l, absd = _hfilter_features(nl)
    zk_c, _ = _hfilter_features(nc)
    u = _modulate(xc, mod[0], nc)
    w_in_t = jnp.swapaxes(w_in, 1, 2)

    for l in range(depth):
        pa = _matmul_t(u, w_in_t, l, 0, 0, SEG1, tm, 512, name="in_proj_a")
        pab = _matmul_t(u, w_in_t, l, SEG1, 0, LANES, tm, LANES, name="in_proj_ab")
        pc = _matmul_t(u, w_in_t, l, SEG1, SEG_AB, SEG_CD, tm, 512, name="in_proj_c")
        pg = _matmul_t(u, w_in_t, l, SEG1 + SEG_CD, SEG_AB, N_BRANCH * D_MODEL, tm, 512, out_dtype=BF16,
                       name="in_proj_g")
        ya = _attn_a(_rope(pa, cos_t, sin_t, nc), attn_sink[l], nl, nc)
        qkvb = _dprep(pa, delta_conv[l], nl, nc, A_HEADS * HEAD_DIM + 2 * A_KV * HEAD_DIM)
        o_f, o_b = _delta(qkvb, pab, delta_a_log[l], delta_dt_bias[l], nl, nc)
        yb = _dpost(o_f, o_b, pa, delta_norm_g[l], nc, SEG1 - BW)
        filt = (hyena_w1[l], hyena_b1[l], hyena_freq[l], hyena_w2[l], hyena_b2[l], hyena_w3[l], hyena_b3[l],
                hyena_w4[l], hyena_b4[l])
        x0c, z = _hconv(pc, hyena_conv[l], nl, nc)
        kr, ki, kn = _kfft(_hfilter(nl, zk_l, absd, *filt), tabs, nl)
        yc_lat = _zfft(z, x0c, kr, ki, kn, hyena_d[l], tabs, nl)
        yc_ctx = _hsmall(z, x0c, _hfilter(nc, zk_c, absd, *filt), hyena_d[l], nl, nc)
        yd = _na(pc, _na_bias_table(na_rel_bias[l]), nl, nc, 3 * BW)
        mrg = _merge(ya, yb, yc_lat, yc_ctx, yd, pg, w_branch, l, 0, nc)
        mix = _matmul(mrg, w_o, l, 0, D_MODEL, tm, 512, name="out_proj")
        x1, u2, lg_t = _ln1(xc, mix, mod[l], ln1_g[l], ln1_b[l], w_router[l], nc)
        idx_l, ws_l, posx_l, posm_l = _select(lg_t, 0, nl, cap_l)
        idx_c, ws_c, posx_c, posm_c = _select(lg_t, nl, nc, cap_c)
        idx = jnp.concatenate([idx_l, idx_c + nl], axis=1)
        wsel = jnp.concatenate([ws_l, ws_c], axis=1)
        y = _ffn(idx, u2, wsel, w_e1, w_e3, w_e2, l, capp)
        posm = jnp.concatenate([posm_l, jnp.where(posm_c >= 0, posm_c + cap_l, -1)], axis=1)
        st_l = posx_l[:, ::nc]
        cnt_l = jnp.concatenate([st_l[:, 1:], jnp.full((N_EXPERTS, 1), cap_l, I32)], axis=1) - st_l
        starts = jnp.concatenate([st_l, posx_c[:, :1] + cap_l], axis=1)
        counts = jnp.concatenate([cnt_l, jnp.full((N_EXPERTS, 1), cap_c, I32)], axis=1)
        xc, u = _combine(starts, counts, posm, y, x1, mod[l], ln2_g[l], ln2_b[l], mod[min(l + 1, depth - 1)], nc)
    return xc[:nl][None]
```
